```python
import math
import jax, jax.numpy as jnp
from jax import lax
import numpy as np

D_MODEL = 2048
BATCH = 2
SEQ = 8192
DEPTH = 2

N_A_LAYERS = DEPTH // 2
N_B_LAYERS = DEPTH - N_A_LAYERS

SSM_EXPAND = 2
D_INNER = SSM_EXPAND * D_MODEL
SSM_HEAD_DIM = 64
SSM_HEADS = D_INNER // SSM_HEAD_DIM
SSM_GROUPS = 8
SSM_HEADS_PER_GROUP = SSM_HEADS // SSM_GROUPS
SSM_STATE = 128
CONV_K = 4
SSD_CHUNK = 256
CONV_DIM = D_INNER + 2 * SSM_GROUPS * SSM_STATE
A_IN_DIM = D_INNER + CONV_DIM + SSM_HEADS
GNORM_GROUPS = SSM_GROUPS

ATT_HEADS = 16
ATT_HEAD_DIM = D_MODEL // ATT_HEADS
ATT_WIDTH = ATT_HEADS * ATT_HEAD_DIM
B_IN_DIM = 2 * ATT_WIDTH
KV_DIM = 2 * ATT_WIDTH
MOBA_BLOCK = 256
MOBA_TOPK = 3
Q_CHUNK = 32
ROPE_THETA = 10000.0
EPS = 1e-5

kernel_name = "yoco_mamba2_moba_hybrid"


def rmsnorm(x, g):
    xf = x.astype(jnp.float32)
    y = xf * lax.rsqrt(jnp.mean(xf * xf, axis=-1, keepdims=True) + EPS)
    return (y * g.astype(jnp.float32)).astype(x.dtype)


def grouped_rmsnorm(y, g, n_groups):
    shp = y.shape
    yf = y.astype(jnp.float32).reshape(shp[:-1] + (n_groups, shp[-1] // n_groups))
    yf = yf * lax.rsqrt(jnp.mean(yf * yf, axis=-1, keepdims=True) + EPS)
    return (yf.reshape(shp) * g.astype(jnp.float32)).astype(y.dtype)


def rope(x):
    s, hd = x.shape[1], x.shape[-1]
    inv_freq = ROPE_THETA ** (-jnp.arange(0, hd, 2, dtype=jnp.float32) / hd)
    ang = jnp.arange(s, dtype=jnp.float32)[:, None] * inv_freq[None, :]
    cos = jnp.cos(ang)[None, :, None, :]
    sin = jnp.sin(ang)[None, :, None, :]
    xf = x.astype(jnp.float32)
    x1, x2 = xf[..., : hd // 2], xf[..., hd // 2:]
    return jnp.concatenate([x1 * cos - x2 * sin, x2 * cos + x1 * sin], axis=-1).astype(x.dtype)


def causal_depthwise_conv(u, w, b):
    c = u.shape[-1]
    out = lax.conv_general_dilated(
        u, w.astype(u.dtype)[:, None, :], window_strides=(1,), padding=[(CONV_K - 1, 0)],
        dimension_numbers=("NWC", "WIO", "NWC"), feature_group_count=c)
    return out + b.astype(u.dtype)


def ssd_chunked_scan(x_dt, a, bm, cm):
    bsz, s = x_dt.shape[0], x_dt.shape[1]
    t = -(-s // SSD_CHUNK) * SSD_CHUNK
    pad = t - s
    nc = t // SSD_CHUNK

    def to_chunks(u):
        u = jnp.pad(u, [(0, 0), (0, pad)] + [(0, 0)] * (u.ndim - 2))
        u = u.reshape((bsz, nc, SSD_CHUNK) + u.shape[2:])
        return jnp.moveaxis(u, 1, 0)

    xs = to_chunks(x_dt.astype(jnp.float32))
    As = to_chunks(a.astype(jnp.float32).reshape(bsz, s, SSM_GROUPS, SSM_HEADS_PER_GROUP))
    Bs = to_chunks(bm.astype(jnp.float32))
    Cs = to_chunks(cm.astype(jnp.float32))
    causal = jnp.tril(jnp.ones((SSD_CHUNK, SSD_CHUNK), dtype=bool))[None, :, :, None, None]

    def step(state, inp):
        xc, ac, bc, cc = inp
        a_cum = jnp.cumsum(ac, axis=1)
        seg = a_cum[:, :, None] - a_cum[:, None, :]
        decay = jnp.exp(jnp.where(causal, seg, -jnp.inf))
        cb = jnp.einsum("blgn,bsgn->blsg", cc, bc)
        y_diag = jnp.einsum("blsgr,bsgrp->blgrp", cb[..., None] * decay, xc)
        y_off = jnp.einsum("blgn,bgrpn->blgrp", cc, state) * jnp.exp(a_cum)[..., None]
        decay_end = jnp.exp(a_cum[:, -1:] - a_cum)
        new_state = state * jnp.exp(a_cum[:, -1])[..., None, None] + jnp.einsum(
            "bsgn,bsgr,bsgrp->bgrpn", bc, decay_end, xc)
        return new_state, y_diag + y_off

    state0 = jnp.zeros((bsz, SSM_GROUPS, SSM_HEADS_PER_GROUP, SSM_HEAD_DIM, SSM_STATE), jnp.float32)
    _, ys = lax.scan(step, state0, (xs, As, Bs, Cs))
    ys = jnp.moveaxis(ys, 0, 1).reshape((bsz, t) + ys.shape[3:])
    return ys[:, :s].astype(x_dt.dtype)


def mamba2_layer(h, w_in, conv_w, conv_b, dt_bias, A_log, D_skip, gnorm_g, w_out):
    bsz, s, _ = h.shape
    proj = h @ w_in
    z = proj[..., :D_INNER]
    xbc = proj[..., D_INNER:D_INNER + CONV_DIM]
    dt = proj[..., D_INNER + CONV_DIM:]
    xbc = jax.nn.silu(causal_depthwise_conv(xbc, conv_w, conv_b))
    xs = xbc[..., :D_INNER].reshape(bsz, s, SSM_GROUPS, SSM_HEADS_PER_GROUP, SSM_HEAD_DIM)
    bm = xbc[..., D_INNER:D_INNER + SSM_GROUPS * SSM_STATE].reshape(bsz, s, SSM_GROUPS, SSM_STATE)
    cm = xbc[..., D_INNER + SSM_GROUPS * SSM_STATE:].reshape(bsz, s, SSM_GROUPS, SSM_STATE)
    dt = jax.nn.softplus(dt.astype(jnp.float32) + dt_bias.astype(jnp.float32))
    A = -jnp.exp(A_log.astype(jnp.float32))
    a = dt * A
    x_dt = xs * dt.reshape(bsz, s, SSM_GROUPS, SSM_HEADS_PER_GROUP)[..., None].astype(xs.dtype)
    y = ssd_chunked_scan(x_dt, a, bm, cm)
    y = y + D_skip.reshape(SSM_GROUPS, SSM_HEADS_PER_GROUP)[:, :, None].astype(xs.dtype) * xs
    y = y.reshape(bsz, s, D_INNER)
    y = grouped_rmsnorm(y * jax.nn.silu(z), gnorm_g, GNORM_GROUPS)
    return y @ w_out


def shared_kv(x, kv_norm_g, w_kv):
    bsz, s, _ = x.shape
    kv = rmsnorm(x, kv_norm_g) @ w_kv
    k = rope(kv[..., :ATT_WIDTH].reshape(bsz, s, ATT_HEADS, ATT_HEAD_DIM))
    v = kv[..., ATT_WIDTH:].reshape(bsz, s, ATT_HEADS, ATT_HEAD_DIM)
    nb = -(-s // MOBA_BLOCK)
    pad = nb * MOBA_BLOCK - s

    def to_blocks(u):
        u = jnp.pad(jnp.transpose(u, (0, 2, 1, 3)), [(0, 0), (0, 0), (0, pad), (0, 0)])
        return u.reshape(bsz, ATT_HEADS, nb, MOBA_BLOCK, ATT_HEAD_DIM)

    kb, vb = to_blocks(k), to_blocks(v)
    kbar = jnp.mean(kb.astype(jnp.float32), axis=3).astype(kb.dtype)
    return kb, vb, kbar


def moba_attention(q, kb, vb, kbar):
    bsz, nh, s, hd = q.shape
    nb = kb.shape[2]
    n_sel = min(MOBA_TOPK, nb)
    scale = ATT_HEAD_DIM ** -0.5
    q_blk = jnp.arange(s) // MOBA_BLOCK
    gate = jnp.einsum("bhsd,bhnd->bhsn", q, kbar).astype(jnp.float32)
    past = jnp.arange(nb)[None, :] < q_blk[:, None]
    gate = jnp.where(past, gate, -jnp.inf)
    _, sel = lax.top_k(gate, n_sel)
    sel_ok = sel < q_blk[:, None]
    bi = jnp.arange(bsz)[:, None, None, None]
    hi = jnp.arange(nh)[None, :, None, None]

    def one_chunk(c):
        start = c * Q_CHUNK
        qc = lax.dynamic_slice_in_dim(q, start, Q_CHUNK, axis=2).astype(jnp.float32)
        idx = lax.dynamic_slice_in_dim(sel, start, Q_CHUNK, axis=2)
        ok = lax.dynamic_slice_in_dim(sel_ok, start, Q_CHUNK, axis=2)
        own = start // MOBA_BLOCK
        k_sel = kb[bi, hi, idx]
        v_sel = vb[bi, hi, idx]
        k_own = lax.dynamic_index_in_dim(kb, own, axis=2, keepdims=False)
        v_own = lax.dynamic_index_in_dim(vb, own, axis=2, keepdims=False)
        s_sel = jnp.einsum("bhqd,bhqjkd->bhqjk", qc, k_sel) * scale
        s_sel = jnp.where(ok[..., None], s_sel, -jnp.inf).reshape(bsz, nh, Q_CHUNK, n_sel * MOBA_BLOCK)
        s_own = jnp.einsum("bhqd,bhkd->bhqk", qc, k_own) * scale
        q_pos = start + jnp.arange(Q_CHUNK)
        k_pos = own * MOBA_BLOCK + jnp.arange(MOBA_BLOCK)
        s_own = jnp.where(k_pos[None, :] <= q_pos[:, None], s_own, -jnp.inf)
        p = jax.nn.softmax(jnp.concatenate([s_sel, s_own], axis=-1), axis=-1)
        p_sel = p[..., : n_sel * MOBA_BLOCK].reshape(bsz, nh, Q_CHUNK, n_sel, MOBA_BLOCK)
        p_own = p[..., n_sel * MOBA_BLOCK:]
        out = jnp.einsum("bhqjk,bhqjkd->bhqd", p_sel, v_sel) + jnp.einsum("bhqk,bhkd->bhqd", p_own, v_own)
        return out.astype(q.dtype)

    outs = lax.map(one_chunk, jnp.arange(s // Q_CHUNK))
    return jnp.transpose(outs, (1, 0, 3, 2, 4)).reshape(bsz, s, nh, hd)


def moba_layer(h, w_in, w_out, kb, vb, kbar):
    bsz, s, _ = h.shape
    proj = h @ w_in
    q = rope(proj[..., :ATT_WIDTH].reshape(bsz, s, ATT_HEADS, ATT_HEAD_DIM))
    z = proj[..., ATT_WIDTH:]
    o = moba_attention(jnp.transpose(q, (0, 2, 1, 3)), kb, vb, kbar).reshape(bsz, s, ATT_WIDTH)
    return (o * jax.nn.silu(z)) @ w_out


def setup_inputs(seed: int = 0) -> dict:
    key = jax.random.key(seed)
    ks = jax.random.split(key, 20)
    f32 = jnp.float32

    def nrm(k, shape, scale):
        return jax.random.normal(k, shape, f32) * scale

    def gain(k, shape):
        return 1.0 + 0.01 * jax.random.normal(k, shape, f32)

    dt = jnp.exp(jax.random.uniform(ks[5], (N_A_LAYERS, SSM_HEADS), f32)
                 * (math.log(0.1) - math.log(0.001)) + math.log(0.001))
    dt_bias = dt + jnp.log(-jnp.expm1(-dt))
    return {
        "x": jax.random.normal(ks[0], (BATCH, SEQ, D_MODEL), f32),
        "a_norm_g": gain(ks[1], (N_A_LAYERS, D_MODEL)),
        "a_w_in": nrm(ks[2], (N_A_LAYERS, D_MODEL, A_IN_DIM), D_MODEL ** -0.5),
        "a_conv_w": nrm(ks[3], (N_A_LAYERS, CONV_K, CONV_DIM), CONV_K ** -0.5),
        "a_conv_b": nrm(ks[4], (N_A_LAYERS, CONV_DIM), 0.01),
        "a_dt_bias": dt_bias,
        "a_A_log": jnp.log(jax.random.uniform(ks[6], (N_A_LAYERS, SSM_HEADS), f32, 1.0, 16.0)),
        "a_D": 1.0 + 0.1 * jax.random.normal(ks[7], (N_A_LAYERS, SSM_HEADS), f32),
        "a_gnorm_g": gain(ks[8], (N_A_LAYERS, D_INNER)),
        "a_w_out": nrm(ks[9], (N_A_LAYERS, D_INNER, D_MODEL), D_INNER ** -0.5),
        "kv_norm_g": gain(ks[10], (D_MODEL,)),
        "w_kv": nrm(ks[11], (D_MODEL, KV_DIM), D_MODEL ** -0.5),
        "b_norm_g": gain(ks[12], (N_B_LAYERS, D_MODEL)),
        "b_w_in": nrm(ks[13], (N_B_LAYERS, D_MODEL, B_IN_DIM), D_MODEL ** -0.5),
        "b_w_out": nrm(ks[14], (N_B_LAYERS, ATT_WIDTH, D_MODEL), ATT_WIDTH ** -0.5),
        "final_norm_g": gain(ks[15], (D_MODEL,)),
    }


def reference(x, a_norm_g, a_w_in, a_conv_w, a_conv_b, a_dt_bias, a_A_log, a_D, a_gnorm_g, a_w_out,
              kv_norm_g, w_kv, b_norm_g, b_w_in, b_w_out, final_norm_g):
    kb = vb = kbar = None
    for i in range(DEPTH):
        if i < N_A_LAYERS:
            h = rmsnorm(x, a_norm_g[i])
            x = x + mamba2_layer(h, a_w_in[i], a_conv_w[i], a_conv_b[i], a_dt_bias[i], a_A_log[i],
                                 a_D[i], a_gnorm_g[i], a_w_out[i])
        else:
            if i == N_A_LAYERS:
                kb, vb, kbar = shared_kv(x, kv_norm_g, w_kv)
            j = i - N_A_LAYERS
            h = rmsnorm(x, b_norm_g[j])
            x = x + moba_layer(h, b_w_in[j], b_w_out[j], kb, vb, kbar)
    return rmsnorm(x, final_norm_g)
```

```python
import functools

import jax
import jax.numpy as jnp
from jax import lax
from jax.experimental import pallas as pl
from jax.experimental.pallas import tpu as pltpu

F32 = jnp.float32
BF16 = jnp.bfloat16

D_MODEL = 2048
SSM_HEAD_DIM = 64
SSM_GROUPS = 8
SSM_STATE = 128
CONV_K = 4
SSD_CHUNK = 256
ATT_HEADS = 16
ATT_HEAD_DIM = 128
MOBA_BLOCK = 256
MOBA_TOPK = 3
ROPE_THETA = 10000.0
EPS = 1e-5

D_INNER = 2 * D_MODEL
SSM_HEADS = D_INNER // SSM_HEAD_DIM
HEADS_PER_GROUP = SSM_HEADS // SSM_GROUPS
GROUP_WIDTH = HEADS_PER_GROUP * SSM_HEAD_DIM
XBC_WIDTH = GROUP_WIDTH + 2 * SSM_STATE
ATT_WIDTH = ATT_HEADS * ATT_HEAD_DIM
HALO = 8

LANES = 128
VMEM_LIMIT = 56 * 1024 * 1024


def _cparams(*sem):
    return pltpu.CompilerParams(dimension_semantics=sem, vmem_limit_bytes=VMEM_LIMIT)


def _rms_scale(x):
    return lax.rsqrt(jnp.mean(x * x, axis=-1, keepdims=True) + EPS)


def _silu(x):
    return x * jax.nn.sigmoid(x)


def _softplus(x):
    return jnp.maximum(x, 0.0) + jnp.log1p(jnp.exp(-jnp.abs(x)))


def _dot(a, b):
    return jnp.dot(a, b, preferred_element_type=F32)


def _dot_nt(a, b):
    return lax.dot_general(a, b, (((1,), (1,)), ((), ())), preferred_element_type=F32)


def _dot_tn(a, b):
    return lax.dot_general(a, b, (((0,), (0,)), ((), ())), preferred_element_type=F32)


def _a_in_proj_kernel(x_ref, g_ref, w_ref, wdt_ref, o_ref, dt_ref, hn_ref):
    @pl.when(pl.program_id(1) == 0)
    def _():
        x = x_ref[...]
        hn = (x * _rms_scale(x) * g_ref[...]).astype(BF16)
        hn_ref[...] = hn
        dt_ref[...] = _dot(hn, wdt_ref[...])

    o_ref[...] = _dot(hn_ref[...], w_ref[...]).astype(o_ref.dtype)


def _a_in_proj(x2d, g, w_main, w_dt, *, tm=512, tn=1024):
    t, d = x2d.shape
    n = w_main.shape[1]
    return pl.pallas_call(
        _a_in_proj_kernel,
        grid=(t // tm, n // tn),
        in_specs=[
            pl.BlockSpec((tm, d), lambda i, j: (i, 0)),
            pl.BlockSpec((1, d), lambda i, j: (0, 0)),
            pl.BlockSpec((d, tn), lambda i, j: (0, j)),
            pl.BlockSpec((d, LANES), lambda i, j: (0, 0)),
        ],
        out_specs=[
            pl.BlockSpec((tm, tn), lambda i, j: (i, j)),
            pl.BlockSpec((tm, LANES), lambda i, j: (i, 0)),
        ],
        out_shape=[
            jax.ShapeDtypeStruct((t, n), BF16),
            jax.ShapeDtypeStruct((t, LANES), F32),
        ],
        scratch_shapes=[pltpu.VMEM((tm, d), BF16)],
        compiler_params=_cparams("parallel", "arbitrary"),
        name="a_in_proj",
    )(x2d, g, w_main, w_dt)


def _split3(a):
    a1 = a.astype(BF16)
    r1 = a - a1.astype(F32)
    a2 = r1.astype(BF16)
    a3 = (r1 - a2.astype(F32)).astype(BF16)
    return a1, a2, a3


def _expand_heads(v, lane):
    rows = v.shape[0]
    parts = []
    for p in range(HEADS_PER_GROUP // 2):
        lo = jnp.broadcast_to(v[:, 2 * p:2 * p + 1], (rows, LANES))
        hi = jnp.broadcast_to(v[:, 2 * p + 1:2 * p + 2], (rows, LANES))
        parts.append(jnp.where(lane < SSM_HEAD_DIM, lo, hi))
    return jnp.concatenate(parts, axis=1)


def _ssd_kernel(z_ref, xbc_ref, dtc_ref, dtr_ref, cw_ref, cb_ref, pc_ref, pr_ref, dexp_ref, gn_ref,
                tril_ref, o_ref, state_ref, halo_ref):
    L = SSD_CHUNK

    @pl.when(pl.program_id(2) == 0)
    def _():
        state_ref[...] = jnp.zeros_like(state_ref)
        halo_ref[...] = jnp.zeros_like(halo_ref)

    raw = xbc_ref[...].astype(F32)
    ext = jnp.concatenate([halo_ref[...], raw], axis=0)
    halo_ref[...] = raw[L - HALO:, :]
    cw = cw_ref[0]
    acc = cb_ref[0] + cw[CONV_K - 1:CONV_K, :] * raw
    for k in range(CONV_K - 1):
        off = HALO - (CONV_K - 1) + k
        acc = acc + cw[k:k + 1, :] * ext[off:off + L, :]
    u = _silu(acc)
    xs = u[:, :GROUP_WIDTH]
    bm = u[:, GROUP_WIDTH:GROUP_WIDTH + SSM_STATE].astype(BF16)
    cm = u[:, GROUP_WIDTH + SSM_STATE:].astype(BF16)

    pc = pc_ref[0]
    pr = pr_ref[0]
    dtc = _softplus(dtc_ref[0, 0] + pc[0:1, :])
    dtr = _softplus(dtr_ref[0, 0] + pr[:, 0:1])
    a_c = dtc * (-jnp.exp(pc[1:2, :]))
    a_r = dtr * (-jnp.exp(pr[:, 1:2]))
    tril = tril_ref[...]
    cum_c = sum(_dot(tril, piece) for piece in _split3(a_c))
    cum_r = sum(_dot_nt(piece, tril) for piece in _split3(a_r))
    e_c = jnp.exp(cum_c)
    dend_c = jnp.exp(cum_c[L - 1:L, :] - cum_c)

    lane = lax.broadcasted_iota(jnp.int32, (L, LANES), 1)
    dt_x = _expand_heads(dtc, lane)
    e_x = _expand_heads(e_c, lane)
    dend_x = _expand_heads(dend_c, lane)

    xdt = xs * dt_x
    xdt_b = xdt.astype(BF16)

    cbm = _dot_nt(cm, bm)
    li = lax.broadcasted_iota(jnp.int32, (L, L), 0)
    si = lax.broadcasted_iota(jnp.int32, (L, L), 1)
    causal = si <= li
    y_parts = []
    for p in range(HEADS_PER_GROUP // 2):
        xp = xdt_b[:, p * LANES:(p + 1) * LANES]
        ys = []
        for r in (2 * p, 2 * p + 1):
            seg = cum_c[:, r:r + 1] - cum_r[r:r + 1, :]
            m = (cbm * jnp.exp(jnp.where(causal, seg, -jnp.inf))).astype(BF16)
            ys.append(_dot(m, xp))
        y_parts.append(jnp.where(lane < SSM_HEAD_DIM, ys[0], ys[1]))
    y = jnp.concatenate(y_parts, axis=1)

    state = state_ref[...]
    y = y + _dot(cm, state.astype(BF16)) * e_x
    state_ref[...] = state * e_x[L - 1:L, :] + _dot_tn(bm, (xdt * dend_x).astype(BF16))

    y = y + dexp_ref[0] * xs
    y = y * _silu(z_ref[...].astype(F32))
    o_ref[...] = (y * _rms_scale(y) * gn_ref[0]).astype(o_ref.dtype)


def _ssd(proj, dtc, dtr, cw, cb, pc, pr, dexp, gn, tril, *, batch, seq):
    L = SSD_CHUNK
    nc = seq // L
    z_block0 = SSM_GROUPS * XBC_WIDTH // GROUP_WIDTH
    return pl.pallas_call(
        _ssd_kernel,
        grid=(batch, SSM_GROUPS, nc),
        in_specs=[
            pl.BlockSpec((L, GROUP_WIDTH), lambda b, g, c: (b * nc + c, z_block0 + g)),
            pl.BlockSpec((L, XBC_WIDTH), lambda b, g, c: (b * nc + c, g)),
            pl.BlockSpec((1, 1, L, HEADS_PER_GROUP), lambda b, g, c: (b, g, c, 0)),
            pl.BlockSpec((1, 1, HEADS_PER_GROUP, L), lambda b, g, c: (b, g, 0, c)),
            pl.BlockSpec((1, CONV_K, XBC_WIDTH), lambda b, g, c: (g, 0, 0)),
            pl.BlockSpec((1, 1, XBC_WIDTH), lambda b, g, c: (g, 0, 0)),
            pl.BlockSpec((1, 2, HEADS_PER_GROUP), lambda b, g, c: (g, 0, 0)),
            pl.BlockSpec((1, HEADS_PER_GROUP, 2), lambda b, g, c: (g, 0, 0)),
            pl.BlockSpec((1, 1, GROUP_WIDTH), lambda b, g, c: (g, 0, 0)),
            pl.BlockSpec((1, 1, GROUP_WIDTH), lambda b, g, c: (g, 0, 0)),
            pl.BlockSpec((L, L), lambda b, g, c: (0, 0)),
        ],
        out_specs=pl.BlockSpec((L, GROUP_WIDTH), lambda b, g, c: (b * nc + c, g)),
        out_shape=jax.ShapeDtypeStruct((batch * seq, D_INNER), BF16),
        scratch_shapes=[pltpu.VMEM((SSM_STATE, GROUP_WIDTH), F32), pltpu.VMEM((HALO, XBC_WIDTH), F32)],
        compiler_params=_cparams("parallel", "parallel", "arbitrary"),
        name="ssd",
    )(proj, proj, dtc, dtr, cw, cb, pc, pr, dexp, gn, tril)


def _proj_res_kernel(y_ref, w_ref, r_ref, o_ref):
    o_ref[...] = r_ref[...] + _dot(y_ref[...], w_ref[...])


def _proj_res(y, w, res, *, tm=512, tn=512):
    t, k = y.shape
    n = w.shape[1]
    return pl.pallas_call(
        _proj_res_kernel,
        grid=(t // tm, n // tn),
        in_specs=[
            pl.BlockSpec((tm, k), lambda i, j: (i, 0)),
            pl.BlockSpec((k, tn), lambda i, j: (0, j)),
            pl.BlockSpec((tm, tn), lambda i, j: (i, j)),
        ],
        out_specs=pl.BlockSpec((tm, tn), lambda i, j: (i, j)),
        out_shape=jax.ShapeDtypeStruct((t, n), F32),
        compiler_params=_cparams("parallel", "parallel"),
        name="a_out_proj",
    )(y, w, res)


def _b_proj_kernel(x_ref, g_ref, w_ref, cos_ref, sin_ref, o_ref, mean_ref, hn_ref, *, n_rope_tiles):
    j = pl.program_id(1)
    tm, tn = o_ref.shape
    nblk = tm // MOBA_BLOCK

    @pl.when(j == 0)
    def _():
        x = x_ref[...]
        hn_ref[...] = (x * _rms_scale(x) * g_ref[...]).astype(BF16)

    acc = _dot(hn_ref[...], w_ref[...])

    @pl.when(j < n_rope_tiles)
    def _():
        cos = cos_ref[...]
        sin = sin_ref[...]
        for h in range(tn // ATT_HEAD_DIM):
            sl = slice(h * ATT_HEAD_DIM, (h + 1) * ATT_HEAD_DIM)
            a = acc[:, sl]
            r = a * cos + pltpu.roll(a, ATT_HEAD_DIM // 2, axis=1) * sin
            o_ref[:, sl] = r.astype(o_ref.dtype)
            for blk in range(nblk):
                rows = slice(blk * MOBA_BLOCK, (blk + 1) * MOBA_BLOCK)
                mean_ref[0, blk:blk + 1, sl] = jnp.mean(r[rows, :], axis=0, keepdims=True)

    @pl.when(j >= n_rope_tiles)
    def _():
        o_ref[...] = acc.astype(o_ref.dtype)
        mean_ref[...] = jnp.zeros_like(mean_ref)


def _b_proj(x2d, g, w, cos, sin, *, seq, tm=512, tn=1024):
    t, d = x2d.shape
    n = w.shape[1]
    n_rope_tiles = ATT_WIDTH // tn
    s_tiles = seq // tm
    nblk = tm // MOBA_BLOCK
    return pl.pallas_call(
        functools.partial(_b_proj_kernel, n_rope_tiles=n_rope_tiles),
        grid=(t // tm, n // tn),
        in_specs=[
            pl.BlockSpec((tm, d), lambda i, j: (i, 0)),
            pl.BlockSpec((1, d), lambda i, j: (0, 0)),
            pl.BlockSpec((d, tn), lambda i, j: (0, j)),
            pl.BlockSpec((tm, ATT_HEAD_DIM), lambda i, j: (i % s_tiles, 0)),
            pl.BlockSpec((tm, ATT_HEAD_DIM), lambda i, j: (i % s_tiles, 0)),
        ],
        out_specs=[
            pl.BlockSpec((tm, tn), lambda i, j: (i, j)),
            pl.BlockSpec((1, nblk, tn), lambda i, j: (i, 0, j)),
        ],
        out_shape=[
            jax.ShapeDtypeStruct((t, n), BF16),
            jax.ShapeDtypeStruct((t // tm, nblk, n), F32),
        ],
        scratch_shapes=[pltpu.VMEM((tm, d), BF16)],
        compiler_params=_cparams("parallel", "arbitrary"),
        name="b_proj",
    )(x2d, g, w, cos, sin)


def _moba_kernel(q_ref, k_ref, v_ref, kbar_ref, z_ref, o_ref, bias_ref):
    qi = pl.program_id(2)
    blk = MOBA_BLOCK
    nb = kbar_ref.shape[0]
    scale = ATT_HEAD_DIM ** -0.5
    q = q_ref[...]

    gate = _dot_nt(kbar_ref[...].astype(BF16), q)
    bidx = lax.broadcasted_iota(jnp.int32, (nb, blk), 0)
    gate = jnp.where(bidx < qi, gate, -jnp.inf)
    sel = jnp.zeros((nb, blk), dtype=jnp.bool_)
    for _ in range(MOBA_TOPK):
        best = jnp.max(gate, axis=0, keepdims=True)
        first = jnp.min(jnp.where(gate == best, bidx, nb), axis=0, keepdims=True)
        pick = (bidx == first) & (best > -jnp.inf)
        sel = sel | pick
        gate = jnp.where(pick, -jnp.inf, gate)
    bias_ref[...] = jnp.where(sel, 0.0, -jnp.inf)

    row0 = pl.multiple_of(qi * blk, blk)
    k_own = k_ref[pl.ds(row0, blk), :]
    v_own = v_ref[pl.ds(row0, blk), :]
    s = _dot_nt(k_own, q) * scale
    kpos = lax.broadcasted_iota(jnp.int32, (blk, blk), 0)
    qpos = lax.broadcasted_iota(jnp.int32, (blk, blk), 1)
    s = jnp.where(kpos <= qpos, s, -jnp.inf)
    m0 = jnp.max(s, axis=0, keepdims=True)
    p = jnp.exp(s - m0)
    l0 = jnp.sum(p, axis=0, keepdims=True)
    acc0 = _dot_tn(v_own, p.astype(BF16))

    def body(j, carry):
        m, l, acc = carry
        r0 = pl.multiple_of(j * blk, blk)
        kj = k_ref[pl.ds(r0, blk), :]
        vj = v_ref[pl.ds(r0, blk), :]
        sj = _dot_nt(kj, q) * scale + bias_ref[pl.ds(j, 1), :]
        m_new = jnp.maximum(m, jnp.max(sj, axis=0, keepdims=True))
        alpha = jnp.exp(m - m_new)
        pj = jnp.exp(sj - m_new)
        l_new = alpha * l + jnp.sum(pj, axis=0, keepdims=True)
        acc_new = alpha * acc + _dot_tn(vj, pj.astype(BF16))
        return m_new, l_new, acc_new

    _, l, acc = lax.fori_loop(0, qi, body, (m0, l0, acc0))
    out = (acc * (1.0 / l)).T
    o_ref[...] = (out * _silu(z_ref[...].astype(F32))).astype(o_ref.dtype)


def _moba(q_z, kv, kbar, *, batch, seq):
    blk = MOBA_BLOCK
    nb = seq // blk
    return pl.pallas_call(
        _moba_kernel,
        grid=(batch, ATT_HEADS, nb),
        in_specs=[
            pl.BlockSpec((blk, ATT_HEAD_DIM), lambda b, h, i: (b * nb + i, h)),
            pl.BlockSpec((seq, ATT_HEAD_DIM), lambda b, h, i: (b, h)),
            pl.BlockSpec((seq, ATT_HEAD_DIM), lambda b, h, i: (b, ATT_HEADS + h)),
            pl.BlockSpec((nb, ATT_HEAD_DIM), lambda b, h, i: (b, h)),
            pl.BlockSpec((blk, ATT_HEAD_DIM), lambda b, h, i: (b * nb + i, ATT_HEADS + h)),
        ],
        out_specs=pl.BlockSpec((blk, ATT_HEAD_DIM), lambda b, h, i: (b * nb + i, h)),
        out_shape=jax.ShapeDtypeStruct((batch * seq, ATT_WIDTH), BF16),
        scratch_shapes=[pltpu.VMEM((nb, blk), F32)],
        compiler_params=_cparams("parallel", "parallel", "arbitrary"),
        name="moba",
    )(q_z, kv, kv, kbar, q_z)


def _final_kernel(o_ref, w_ref, r_ref, g_ref, out_ref):
    x = r_ref[...] + _dot(o_ref[...], w_ref[...])
    out_ref[...] = x * _rms_scale(x) * g_ref[...]


def _final(o, w, res, g, *, tm=512):
    t, k = o.shape
    n = w.shape[1]
    return pl.pallas_call(
        _final_kernel,
        grid=(t // tm,),
        in_specs=[
            pl.BlockSpec((tm, k), lambda i: (i, 0)),
            pl.BlockSpec((k, n), lambda i: (0, 0)),
            pl.BlockSpec((tm, n), lambda i: (i, 0)),
            pl.BlockSpec((1, n), lambda i: (0, 0)),
        ],
        out_specs=pl.BlockSpec((tm, n), lambda i: (i, 0)),
        out_shape=jax.ShapeDtypeStruct((t, n), F32),
        compiler_params=_cparams("parallel"),
        name="final",
    )(o, w, res, g)


def _rope_tables(seq):
    hd = ATT_HEAD_DIM
    inv_freq = ROPE_THETA ** (-jnp.arange(0, hd, 2, dtype=F32) / hd)
    ang = jnp.arange(seq, dtype=F32)[:, None] * inv_freq[None, :]
    cos, sin = jnp.cos(ang), jnp.sin(ang)
    return jnp.concatenate([cos, cos], axis=1), jnp.concatenate([-sin, sin], axis=1)


def kernel(x, a_norm_g, a_w_in, a_conv_w, a_conv_b, a_dt_bias, a_A_log, a_D, a_gnorm_g, a_w_out,
           kv_norm_g, w_kv, b_norm_g, b_w_in, b_w_out, final_norm_g):
    batch, seq, d = x.shape
    assert d == D_MODEL and seq % SSD_CHUNK == 0 and seq % MOBA_BLOCK == 0
    assert a_w_in.shape[0] == 1 and b_w_in.shape[0] == 1
    t = batch * seq
    G, R, N = SSM_GROUPS, HEADS_PER_GROUP, SSM_STATE
    x2d = x.reshape(t, d)

    w_in = a_w_in[0]
    w_z = w_in[:, :D_INNER]
    w_x = w_in[:, D_INNER:2 * D_INNER].reshape(d, G, GROUP_WIDTH)
    w_b = w_in[:, 2 * D_INNER:2 * D_INNER + G * N].reshape(d, G, N)
    w_c = w_in[:, 2 * D_INNER + G * N:2 * D_INNER + 2 * G * N].reshape(d, G, N)
    w_xbc = jnp.concatenate([w_x, w_b, w_c], axis=2).reshape(d, G * XBC_WIDTH)
    w_main = jnp.concatenate([w_xbc, w_z], axis=1).astype(BF16)
    w_dt = jnp.pad(w_in[:, 2 * D_INNER + 2 * G * N:], ((0, 0), (0, LANES - SSM_HEADS))).astype(BF16)

    def group_major(p):
        rows = p.shape[0]
        px = p[:, :D_INNER].reshape(rows, G, GROUP_WIDTH)
        pb = p[:, D_INNER:D_INNER + G * N].reshape(rows, G, N)
        pcc = p[:, D_INNER + G * N:].reshape(rows, G, N)
        return jnp.transpose(jnp.concatenate([px, pb, pcc], axis=2), (1, 0, 2))

    cw = group_major(a_conv_w[0])
    cb = group_major(a_conv_b[0][None, :])
    dt_bias = a_dt_bias[0].reshape(G, R)
    a_log = a_A_log[0].reshape(G, R)
    pc = jnp.stack([dt_bias, a_log], axis=1)
    pr = jnp.stack([dt_bias, a_log], axis=2)
    dexp = jnp.repeat(a_D[0], SSM_HEAD_DIM).reshape(G, 1, GROUP_WIDTH)
    gn = a_gnorm_g[0].reshape(G, 1, GROUP_WIDTH)
    tril = jnp.tril(jnp.ones((SSD_CHUNK, SSD_CHUNK), BF16))

    proj, dt_raw = _a_in_proj(x2d, a_norm_g[0][None, :], w_main, w_dt)
    dt4 = dt_raw[:, :SSM_HEADS].reshape(batch, seq, G, R)
    dtc = jnp.transpose(dt4, (0, 2, 1, 3))
    dtr = jnp.transpose(dt4, (0, 2, 3, 1))
    y = _ssd(proj, dtc, dtr, cw, cb, pc, pr, dexp, gn, tril, batch=batch, seq=seq)
    x1 = _proj_res(y, a_w_out[0].astype(BF16), x2d)

    cos, sin = _rope_tables(seq)
    kv, kmean = _b_proj(x1, kv_norm_g[None, :], w_kv.astype(BF16), cos, sin, seq=seq)
    kbar = kmean.reshape(t // MOBA_BLOCK, 2 * ATT_WIDTH)
    q_z, _ = _b_proj(x1, b_norm_g[0][None, :], b_w_in[0].astype(BF16), cos, sin, seq=seq)
    o = _moba(q_z, kv, kbar, batch=batch, seq=seq)
    out = _final(o, b_w_out[0].astype(BF16), x1, final_norm_g[None, :])
    return out.reshape(batch, seq, d)
```

```python
import functools

import jax
import jax.numpy as jnp
from jax import lax
from jax.experimental import pallas as pl
from jax.experimental.pallas import tpu as pltpu

F32 = jnp.float32
BF16 = jnp.bfloat16

D_MODEL = 2048
SSM_HEAD_DIM = 64
SSM_GROUPS = 8
SSM_STATE = 128
CONV_K = 4
SSD_CHUNK = 256
ATT_HEADS = 16
ATT_HEAD_DIM = 128
MOBA_BLOCK = 256
MOBA_TOPK = 3
ROPE_THETA = 10000.0
EPS = 1e-5

D_INNER = 2 * D_MODEL
SSM_HEADS = D_INNER // SSM_HEAD_DIM
HEADS_PER_GROUP = SSM_HEADS // SSM_GROUPS
GROUP_WIDTH = HEADS_PER_GROUP * SSM_HEAD_DIM
XBC_WIDTH = GROUP_WIDTH + 2 * SSM_STATE
ATT_WIDTH = ATT_HEADS * ATT_HEAD_DIM
LOG2E = 1.4426950408889634
QK_SCALE_LOG2E = (ATT_HEAD_DIM ** -0.5) * LOG2E
HALO = 8

LANES = 128
VMEM_LIMIT = 60 * 1024 * 1024


def _cparams(*sem):
    return pltpu.CompilerParams(dimension_semantics=sem, vmem_limit_bytes=VMEM_LIMIT)


def _rms_scale(x):
    return lax.rsqrt(jnp.mean(x * x, axis=-1, keepdims=True) + EPS)


def _silu(x):
    return x * jax.nn.sigmoid(x)


def _softplus(x):
    return jnp.maximum(x, 0.0) + jnp.log1p(jnp.exp(-jnp.abs(x)))


def _dot(a, b):
    return jnp.dot(a, b, preferred_element_type=F32)


def _dot_nt(a, b):
    return lax.dot_general(a, b, (((1,), (1,)), ((), ())), preferred_element_type=F32)


def _dot_tn(a, b):
    return lax.dot_general(a, b, (((0,), (0,)), ((), ())), preferred_element_type=F32)


def _a_in_proj_kernel(x_ref, g_ref, w_ref, wdt_ref, o_ref, dt_ref, hn_ref):
    @pl.when(pl.program_id(1) == 0)
    def _():
        x = x_ref[...]
        hn = (x * _rms_scale(x) * g_ref[...]).astype(BF16)
        hn_ref[...] = hn
        dt_ref[...] = _dot(hn, wdt_ref[...])

    o_ref[...] = _dot(hn_ref[...], w_ref[...]).astype(o_ref.dtype)


def _a_in_proj(x2d, g, w_main, w_dt, *, tm=1024, tn=1024):
    t, d = x2d.shape
    n = w_main.shape[1]
    return pl.pallas_call(
        _a_in_proj_kernel,
        grid=(t // tm, n // tn),
        in_specs=[
            pl.BlockSpec((tm, d), lambda i, j: (i, 0)),
            pl.BlockSpec((1, d), lambda i, j: (0, 0)),
            pl.BlockSpec((d, tn), lambda i, j: (0, j)),
            pl.BlockSpec((d, LANES), lambda i, j: (0, 0)),
        ],
        out_specs=[
            pl.BlockSpec((tm, tn), lambda i, j: (i, j)),
            pl.BlockSpec((tm, LANES), lambda i, j: (i, 0)),
        ],
        out_shape=[
            jax.ShapeDtypeStruct((t, n), BF16),
            jax.ShapeDtypeStruct((t, LANES), F32),
        ],
        scratch_shapes=[pltpu.VMEM((tm, d), BF16)],
        compiler_params=_cparams("parallel", "arbitrary"),
        name="a_in_proj",
    )(x2d, g, w_main, w_dt)


def _split3(a):
    a1 = a.astype(BF16)
    r1 = a - a1.astype(F32)
    a2 = r1.astype(BF16)
    a3 = (r1 - a2.astype(F32)).astype(BF16)
    return a1, a2, a3


def _ssd_kernel(z_ref, xbc_ref, dtc_ref, dtr_ref, cw_ref, cb_ref, pc_ref, pr_ref, dexp_ref, gn_ref,
                tril_ref, expand_ref, o_ref, state_ref, ext_ref):
    L = SSD_CHUNK

    @pl.when(pl.program_id(2) == 0)
    def _():
        state_ref[...] = jnp.zeros_like(state_ref)
        ext_ref[0:HALO, :] = jnp.zeros((HALO, XBC_WIDTH), F32)

    raw = xbc_ref[...].astype(F32)
    ext_ref[HALO:HALO + L, :] = raw
    cw = cw_ref[0]
    acc = cb_ref[0] + cw[CONV_K - 1:CONV_K, :] * raw
    for k in range(CONV_K - 1):
        off = HALO - (CONV_K - 1) + k
        acc = acc + cw[k:k + 1, :] * ext_ref[off:off + L, :]
    ext_ref[0:HALO, :] = ext_ref[L:L + HALO, :]
    u = _silu(acc)
    xs = u[:, :GROUP_WIDTH]
    bm = u[:, GROUP_WIDTH:GROUP_WIDTH + SSM_STATE].astype(BF16)
    cm = u[:, GROUP_WIDTH + SSM_STATE:].astype(BF16)

    pc = pc_ref[0]
    pr = pr_ref[0]
    dtc = _softplus(dtc_ref[0, 0] + pc[0:1, :])
    dtr = _softplus(dtr_ref[0, 0] + pr[:, 0:1])
    a_c = dtc * (-jnp.exp(pc[1:2, :]))
    a_r = dtr * (-jnp.exp(pr[:, 1:2]))
    tril = tril_ref[...]
    cum_c = sum(_dot(tril, piece) for piece in _split3(a_c * LOG2E))
    cum_r = sum(_dot_nt(piece, tril) for piece in _split3(a_r * LOG2E))
    e_c = jnp.exp2(cum_c)
    dend_c = jnp.exp2(cum_c[L - 1:L, :] - cum_c)

    expand = expand_ref[...]
    dt_x = _dot(dtc.astype(BF16), expand)
    e_x = _dot(e_c.astype(BF16), expand)
    dend_x = _dot(dend_c.astype(BF16), expand)
    e_last_x = sum(_dot(piece, expand) for piece in _split3(e_c[L - HALO:, :]))[HALO - 1:HALO, :]
    lane = lax.broadcasted_iota(jnp.int32, (L, LANES), 1)

    xdt = xs * dt_x
    xdt_b = xdt.astype(BF16)

    cbm = _dot_nt(cm, bm)
    li = lax.broadcasted_iota(jnp.int32, (L, L), 0)
    si = lax.broadcasted_iota(jnp.int32, (L, L), 1)
    causal = si <= li
    y_parts = []
    for p in range(HEADS_PER_GROUP // 2):
        xp = xdt_b[:, p * LANES:(p + 1) * LANES]
        ys = []
        for r in (2 * p, 2 * p + 1):
            seg = cum_c[:, r:r + 1] - cum_r[r:r + 1, :]
            m = (cbm * jnp.exp2(jnp.where(causal, seg, -jnp.inf))).astype(BF16)
            ys.append(_dot(m, xp))
        y_parts.append(jnp.where(lane < SSM_HEAD_DIM, ys[0], ys[1]))
    y = jnp.concatenate(y_parts, axis=1)

    state = state_ref[...]
    y = y + _dot(cm, state.astype(BF16)) * e_x
    state_ref[...] = state * e_last_x + _dot_tn(bm, (xdt * dend_x).astype(BF16))

    y = y + dexp_ref[0] * xs
    y = y * _silu(z_ref[...].astype(F32))
    o_ref[...] = (y * _rms_scale(y) * gn_ref[0]).astype(o_ref.dtype)


def _ssd(proj, dtc, dtr, cw, cb, pc, pr, dexp, gn, tril, expand, *, batch, seq):
    L = SSD_CHUNK
    nc = seq // L
    z_block0 = SSM_GROUPS * XBC_WIDTH // GROUP_WIDTH
    return pl.pallas_call(
        _ssd_kernel,
        grid=(batch, SSM_GROUPS, nc),
        in_specs=[
            pl.BlockSpec((L, GROUP_WIDTH), lambda b, g, c: (b * nc + c, z_block0 + g)),
            pl.BlockSpec((L, XBC_WIDTH), lambda b, g, c: (b * nc + c, g)),
            pl.BlockSpec((1, 1, L, HEADS_PER_GROUP), lambda b, g, c: (b, g, c, 0)),
            pl.BlockSpec((1, 1, HEADS_PER_GROUP, L), lambda b, g, c: (b, g, 0, c)),
            pl.BlockSpec((1, CONV_K, XBC_WIDTH), lambda b, g, c: (g, 0, 0)),
            pl.BlockSpec((1, 1, XBC_WIDTH), lambda b, g, c: (g, 0, 0)),
            pl.BlockSpec((1, 2, HEADS_PER_GROUP), lambda b, g, c: (g, 0, 0)),
            pl.BlockSpec((1, HEADS_PER_GROUP, 2), lambda b, g, c: (g, 0, 0)),
            pl.BlockSpec((1, 1, GROUP_WIDTH), lambda b, g, c: (g, 0, 0)),
            pl.BlockSpec((1, 1, GROUP_WIDTH), lambda b, g, c: (g, 0, 0)),
            pl.BlockSpec((L, L), lambda b, g, c: (0, 0)),
            pl.BlockSpec((HEADS_PER_GROUP, GROUP_WIDTH), lambda b, g, c: (0, 0)),
        ],
        out_specs=pl.BlockSpec((L, GROUP_WIDTH), lambda b, g, c: (b * nc + c, g)),
        out_shape=jax.ShapeDtypeStruct((batch * seq, D_INNER), BF16),
        scratch_shapes=[pltpu.VMEM((SSM_STATE, GROUP_WIDTH), F32),
                        pltpu.VMEM((HALO + L, XBC_WIDTH), F32)],
        compiler_params=_cparams("parallel", "parallel", "arbitrary"),
        name="ssd",
    )(proj, proj, dtc, dtr, cw, cb, pc, pr, dexp, gn, tril, expand)


def _proj_res_kernel(y_ref, w_ref, r_ref, o_ref):
    o_ref[...] = r_ref[...] + _dot(y_ref[...], w_ref[...])


def _proj_res(y, w, res, *, tm=1024, tn=512):
    t, k = y.shape
    n = w.shape[1]
    return pl.pallas_call(
        _proj_res_kernel,
        grid=(t // tm, n // tn),
        in_specs=[
            pl.BlockSpec((tm, k), lambda i, j: (i, 0)),
            pl.BlockSpec((k, tn), lambda i, j: (0, j)),
            pl.BlockSpec((tm, tn), lambda i, j: (i, j)),
        ],
        out_specs=pl.BlockSpec((tm, tn), lambda i, j: (i, j)),
        out_shape=jax.ShapeDtypeStruct((t, n), F32),
        compiler_params=_cparams("parallel", "parallel"),
        name="a_out_proj",
    )(y, w, res)


def _b_proj_kernel(x_ref, g_ref, w_ref, cos_ref, sin_ref, o_ref, mean_ref, hn_ref, *, tiles_per_section):
    j = pl.program_id(1)
    tm, tn = o_ref.shape
    nblk = tm // MOBA_BLOCK
    section = j // tiles_per_section

    @pl.when(j == 0)
    def _():
        x = x_ref[...]
        xn = x * _rms_scale(x)
        hn_ref[0] = (xn * g_ref[0:1, :]).astype(BF16)
        hn_ref[1] = (xn * g_ref[1:2, :]).astype(BF16)

    acc = _dot(hn_ref[section // 2], w_ref[...])

    def rope(a):
        return a * cos_ref[...] + pltpu.roll(a, ATT_HEAD_DIM // 2, axis=1) * sin_ref[...]

    heads = [slice(h * ATT_HEAD_DIM, (h + 1) * ATT_HEAD_DIM) for h in range(tn // ATT_HEAD_DIM)]

    @pl.when(section == 0)
    def _():
        for sl in heads:
            r = rope(acc[:, sl])
            o_ref[:, sl] = r.astype(o_ref.dtype)
            for blk in range(nblk):
                rows = slice(blk * MOBA_BLOCK, (blk + 1) * MOBA_BLOCK)
                mean_ref[0, blk:blk + 1, sl] = jnp.mean(r[rows, :], axis=0, keepdims=True)

    @pl.when(section == 2)
    def _():
        for sl in heads:
            o_ref[:, sl] = (rope(acc[:, sl]) * QK_SCALE_LOG2E).astype(o_ref.dtype)

    @pl.when(section % 2 == 1)
    def _():
        o_ref[...] = acc.astype(o_ref.dtype)


def _b_proj(x2d, g2, w, cos, sin, *, seq, tm=1024, tn=1024):
    t, d = x2d.shape
    n = w.shape[1]
    tiles_per_section = ATT_WIDTH // tn
    s_tiles = seq // tm
    nblk = tm // MOBA_BLOCK
    return pl.pallas_call(
        functools.partial(_b_proj_kernel, tiles_per_section=tiles_per_section),
        grid=(t // tm, n // tn),
        in_specs=[
            pl.BlockSpec((tm, d), lambda i, j: (i, 0)),
            pl.BlockSpec((2, d), lambda i, j: (0, 0)),
            pl.BlockSpec((d, tn), lambda i, j: (0, j)),
            pl.BlockSpec((tm, ATT_HEAD_DIM), lambda i, j: (i % s_tiles, 0)),
            pl.BlockSpec((tm, ATT_HEAD_DIM), lambda i, j: (i % s_tiles, 0)),
        ],
        out_specs=[
            pl.BlockSpec((tm, tn), lambda i, j: (i, j)),
            pl.BlockSpec((1, nblk, tn), lambda i, j: (i, 0, jnp.minimum(j, tiles_per_section - 1))),
        ],
        out_shape=[
            jax.ShapeDtypeStruct((t, n), BF16),
            jax.ShapeDtypeStruct((t // tm, nblk, ATT_WIDTH), F32),
        ],
        scratch_shapes=[pltpu.VMEM((2, tm, d), BF16)],
        compiler_params=_cparams("parallel", "arbitrary"),
        name="b_proj",
    )(x2d, g2, w, cos, sin)


def _moba_kernel(q_ref, k_ref, v_ref, kbar_ref, z_ref, o_ref, off_ref, acc_ref, sa_ref, sb_ref, *, heads):
    qi = pl.program_id(2)
    blk = MOBA_BLOCK
    hd = ATT_HEAD_DIM
    nb = kbar_ref.shape[0]
    cols = [slice(h * hd, (h + 1) * hd) for h in range(heads)]
    qs = [q_ref[:, cs] for cs in cols]

    row0 = pl.multiple_of(qi * blk, blk)
    bidx = lax.broadcasted_iota(jnp.int32, (nb, blk), 0)
    kpos = lax.broadcasted_iota(jnp.int32, (blk, blk), 0)
    qpos = lax.broadcasted_iota(jnp.int32, (blk, blk), 1)
    m_init, l_init = [], []
    for h in range(heads):
        gate = _dot_nt(kbar_ref[:, cols[h]].astype(BF16), qs[h])
        gate = jnp.where(bidx < qi, gate, -jnp.inf)
        sel = jnp.zeros((nb, blk), dtype=jnp.bool_)
        for _ in range(MOBA_TOPK):
            best = jnp.max(gate, axis=0, keepdims=True)
            first = jnp.min(jnp.where(gate == best, bidx, nb), axis=0, keepdims=True)
            pick = (bidx == first) & (best > -jnp.inf)
            sel = sel | pick
            gate = jnp.where(pick, -jnp.inf, gate)
        off_ref[h] = jnp.where(sel, 0.0, jnp.inf)

        s = _dot_nt(k_ref[pl.ds(row0, blk), cols[h]], qs[h])
        s = jnp.where(kpos <= qpos, s, -jnp.inf)
        m0 = jnp.max(s, axis=0, keepdims=True)
        p = jnp.exp2(s - m0)
        m_init.append(m0)
        l_init.append(jnp.sum(p, axis=0, keepdims=True))
        acc_ref[h] = _dot_tn(v_ref[pl.ds(row0, blk), cols[h]], p.astype(BF16))

    def issue_scores(j, dst_ref):
        r = pl.multiple_of(j * blk, blk)
        for h in range(heads):
            dst_ref[h] = _dot_nt(k_ref[pl.ds(r, blk), cols[h]], qs[h])

    def half_step(j, ms, ls, cur_ref, nxt_ref):
        jc = jnp.minimum(j, qi - 1)
        r0 = pl.multiple_of(jc * blk, blk)
        live = j < qi
        offs = [jnp.where(live, off_ref[h, pl.ds(jc, 1), :], jnp.inf) for h in range(heads)]
        issue_scores(jnp.minimum(j + 1, qi - 1), nxt_ref)
        new_ms, new_ls, alphas, pvs = [], [], [], []
        for h in range(heads):
            sj = cur_ref[h]
            m_new = jnp.maximum(ms[h], jnp.max(sj, axis=0, keepdims=True) - offs[h])
            alpha = jnp.exp2(ms[h] - m_new)
            pj = jnp.exp2(sj - (m_new + offs[h]))
            new_ms.append(m_new)
            new_ls.append(alpha * ls[h] + jnp.sum(pj, axis=0, keepdims=True))
            alphas.append(alpha)
            pvs.append(_dot_tn(v_ref[pl.ds(r0, blk), cols[h]], pj.astype(BF16)))
        new_accs = [alphas[h] * acc_ref[h] + pvs[h] for h in range(heads)]
        for h in range(heads):
            acc_ref[h] = new_accs[h]
        return tuple(new_ms), tuple(new_ls)

    def body(jj, carry):
        ms, ls = carry
        ms, ls = half_step(2 * jj, ms, ls, sa_ref, sb_ref)
        return half_step(2 * jj + 1, ms, ls, sb_ref, sa_ref)

    @pl.when(qi > 0)
    def _():
        issue_scores(0, sa_ref)

    _, ls = lax.fori_loop(0, (qi + 1) // 2, body, (tuple(m_init), tuple(l_init)))
    for h in range(heads):
        out = (acc_ref[h] * (1.0 / ls[h])).T
        o_ref[:, cols[h]] = (out * _silu(z_ref[:, cols[h]].astype(F32))).astype(o_ref.dtype)


def _moba(kvqz, kbar, *, batch, seq, heads=4):
    blk = MOBA_BLOCK
    nb = seq // blk
    w = heads * ATT_HEAD_DIM
    ngroups = ATT_HEADS // heads
    return pl.pallas_call(
        functools.partial(_moba_kernel, heads=heads),
        grid=(batch, ngroups, nb),
        in_specs=[
            pl.BlockSpec((blk, w), lambda b, h, i: (b * nb + i, 2 * ngroups + h)),
            pl.BlockSpec((seq, w), lambda b, h, i: (b, h)),
            pl.BlockSpec((seq, w), lambda b, h, i: (b, ngroups + h)),
            pl.BlockSpec((nb, w), lambda b, h, i: (b, h)),
            pl.BlockSpec((blk, w), lambda b, h, i: (b * nb + i, 3 * ngroups + h)),
        ],
        out_specs=pl.BlockSpec((blk, w), lambda b, h, i: (b * nb + i, h)),
        out_shape=jax.ShapeDtypeStruct((batch * seq, ATT_WIDTH), BF16),
        scratch_shapes=[pltpu.VMEM((heads, nb, blk), F32),
                        pltpu.VMEM((heads, ATT_HEAD_DIM, blk), F32),
                        pltpu.VMEM((heads, blk, blk), F32),
                        pltpu.VMEM((heads, blk, blk), F32)],
        compiler_params=_cparams("parallel", "parallel", "arbitrary"),
        name="moba",
    )(kvqz, kvqz, kvqz, kbar, kvqz)


def _final_kernel(o_ref, w_ref, r_ref, g_ref, out_ref):
    x = r_ref[...] + _dot(o_ref[...], w_ref[...])
    out_ref[...] = x * _rms_scale(x) * g_ref[...]


def _final(o, w, res, g, *, tm=512):
    t, k = o.shape
    n = w.shape[1]
    return pl.pallas_call(
        _final_kernel,
        grid=(t // tm,),
        in_specs=[
            pl.BlockSpec((tm, k), lambda i: (i, 0)),
            pl.BlockSpec((k, n), lambda i: (0, 0)),
            pl.BlockSpec((tm, n), lambda i: (i, 0)),
            pl.BlockSpec((1, n), lambda i: (0, 0)),
        ],
        out_specs=pl.BlockSpec((tm, n), lambda i: (i, 0)),
        out_shape=jax.ShapeDtypeStruct((t, n), F32),
        compiler_params=_cparams("parallel"),
        name="final",
    )(o, w, res, g)


def _rope_tables(seq):
    hd = ATT_HEAD_DIM
    inv_freq = ROPE_THETA ** (-jnp.arange(0, hd, 2, dtype=F32) / hd)
    ang = jnp.arange(seq, dtype=F32)[:, None] * inv_freq[None, :]
    cos, sin = jnp.cos(ang), jnp.sin(ang)
    return jnp.concatenate([cos, cos], axis=1), jnp.concatenate([-sin, sin], axis=1)


def kernel(x, a_norm_g, a_w_in, a_conv_w, a_conv_b, a_dt_bias, a_A_log, a_D, a_gnorm_g, a_w_out,
           kv_norm_g, w_kv, b_norm_g, b_w_in, b_w_out, final_norm_g):
    batch, seq, d = x.shape
    assert d == D_MODEL and seq % SSD_CHUNK == 0 and seq % MOBA_BLOCK == 0
    assert a_w_in.shape[0] == 1 and b_w_in.shape[0] == 1
    t = batch * seq
    G, R, N = SSM_GROUPS, HEADS_PER_GROUP, SSM_STATE
    x2d = x.reshape(t, d)

    w_in = a_w_in[0]
    w_z = w_in[:, :D_INNER]
    w_x = w_in[:, D_INNER:2 * D_INNER].reshape(d, G, GROUP_WIDTH)
    w_b = w_in[:, 2 * D_INNER:2 * D_INNER + G * N].reshape(d, G, N)
    w_c = w_in[:, 2 * D_INNER + G * N:2 * D_INNER + 2 * G * N].reshape(d, G, N)
    w_xbc = jnp.concatenate([w_x, w_b, w_c], axis=2).reshape(d, G * XBC_WIDTH)
    w_main = jnp.concatenate([w_xbc, w_z], axis=1).astype(BF16)
    w_dt = jnp.pad(w_in[:, 2 * D_INNER + 2 * G * N:], ((0, 0), (0, LANES - SSM_HEADS))).astype(BF16)

    def group_major(p):
        rows = p.shape[0]
        px = p[:, :D_INNER].reshape(rows, G, GROUP_WIDTH)
        pb = p[:, D_INNER:D_INNER + G * N].reshape(rows, G, N)
        pcc = p[:, D_INNER + G * N:].reshape(rows, G, N)
        return jnp.transpose(jnp.concatenate([px, pb, pcc], axis=2), (1, 0, 2))

    cw = group_major(a_conv_w[0])
    cb = group_major(a_conv_b[0][None, :])
    dt_bias = a_dt_bias[0].reshape(G, R)
    a_log = a_A_log[0].reshape(G, R)
    pc = jnp.stack([dt_bias, a_log], axis=1)
    pr = jnp.stack([dt_bias, a_log], axis=2)
    dexp = jnp.repeat(a_D[0], SSM_HEAD_DIM).reshape(G, 1, GROUP_WIDTH)
    gn = a_gnorm_g[0].reshape(G, 1, GROUP_WIDTH)
    tril = jnp.tril(jnp.ones((SSD_CHUNK, SSD_CHUNK), BF16))
    expand = jnp.repeat(jnp.eye(R, dtype=BF16), SSM_HEAD_DIM, axis=1)

    proj, dt_raw = _a_in_proj(x2d, a_norm_g[0][None, :], w_main, w_dt)
    dt4 = dt_raw[:, :SSM_HEADS].reshape(batch, seq, G, R)
    dtc = jnp.transpose(dt4, (0, 2, 1, 3))
    dtr = jnp.transpose(dt4, (0, 2, 3, 1))
    y = _ssd(proj, dtc, dtr, cw, cb, pc, pr, dexp, gn, tril, expand, batch=batch, seq=seq)
    x1 = _proj_res(y, a_w_out[0].astype(BF16), x2d)

    cos, sin = _rope_tables(seq)
    g2 = jnp.stack([kv_norm_g, b_norm_g[0]], axis=0)
    w_b = jnp.concatenate([w_kv, b_w_in[0]], axis=1).astype(BF16)
    kvqz, kmean = _b_proj(x1, g2, w_b, cos, sin, seq=seq)
    kbar = kmean.reshape(t // MOBA_BLOCK, ATT_WIDTH)
    o = _moba(kvqz, kbar, batch=batch, seq=seq)
    out = _final(o, b_w_out[0].astype(BF16), x1, final_norm_g[None, :])
    return out.reshape(batch, seq, d)
```

```python
import functools

import jax
import jax.numpy as jnp
from jax import lax
from jax.experimental import pallas as pl
from jax.experimental.pallas import tpu as pltpu

F32 = jnp.float32
BF16 = jnp.bfloat16

D_MODEL = 2048
SSM_HEAD_DIM = 64
SSM_GROUPS = 8
SSM_STATE = 128
CONV_K = 4
SSD_CHUNK = 256
ATT_HEADS = 16
ATT_HEAD_DIM = 128
MOBA_BLOCK = 256
MOBA_TOPK = 3
ROPE_THETA = 10000.0
EPS = 1e-5

D_INNER = 2 * D_MODEL
SSM_HEADS = D_INNER // SSM_HEAD_DIM
HEADS_PER_GROUP = SSM_HEADS // SSM_GROUPS
GROUP_WIDTH = HEADS_PER_GROUP * SSM_HEAD_DIM
XBC_WIDTH = GROUP_WIDTH + 2 * SSM_STATE
ATT_WIDTH = ATT_HEADS * ATT_HEAD_DIM
LOG2E = 1.4426950408889634
QK_SCALE_LOG2E = (ATT_HEAD_DIM ** -0.5) * LOG2E
HALO = 8

LANES = 128
VMEM_LIMIT = 60 * 1024 * 1024


def _cparams(*sem):
    return pltpu.CompilerParams(dimension_semantics=sem, vmem_limit_bytes=VMEM_LIMIT)


def _rms_scale(x):
    return lax.rsqrt(jnp.mean(x * x, axis=-1, keepdims=True) + EPS)


def _silu(x):
    return x * jax.nn.sigmoid(x)


def _softplus(x):
    return jnp.maximum(x, 0.0) + jnp.log1p(jnp.exp(-jnp.abs(x)))


def _dot(a, b):
    return jnp.dot(a, b, preferred_element_type=F32)


def _dot_nt(a, b):
    return lax.dot_general(a, b, (((1,), (1,)), ((), ())), preferred_element_type=F32)


def _dot_tn(a, b):
    return lax.dot_general(a, b, (((0,), (0,)), ((), ())), preferred_element_type=F32)


def _a_in_proj_kernel(x_ref, g_ref, w_ref, wdt_ref, o_ref, dt_ref, hn_ref):
    @pl.when(pl.program_id(1) == 0)
    def _():
        x = x_ref[...]
        hn = (x * _rms_scale(x) * g_ref[...]).astype(BF16)
        hn_ref[...] = hn
        dt_ref[...] = _dot(hn, wdt_ref[...])

    o_ref[...] = _dot(hn_ref[...], w_ref[...]).astype(o_ref.dtype)


def _a_in_proj(x2d, g, w_main, w_dt, *, tm=1024, tn=1024):
    t, d = x2d.shape
    n = w_main.shape[1]
    return pl.pallas_call(
        _a_in_proj_kernel,
        grid=(t // tm, n // tn),
        in_specs=[
            pl.BlockSpec((tm, d), lambda i, j: (i, 0)),
            pl.BlockSpec((1, d), lambda i, j: (0, 0)),
            pl.BlockSpec((d, tn), lambda i, j: (0, j)),
            pl.BlockSpec((d, LANES), lambda i, j: (0, 0)),
        ],
        out_specs=[
            pl.BlockSpec((tm, tn), lambda i, j: (i, j)),
            pl.BlockSpec((tm, LANES), lambda i, j: (i, 0)),
        ],
        out_shape=[
            jax.ShapeDtypeStruct((t, n), BF16),
            jax.ShapeDtypeStruct((t, LANES), F32),
        ],
        scratch_shapes=[pltpu.VMEM((tm, d), BF16)],
        compiler_params=_cparams("parallel", "arbitrary"),
        name="a_in_proj",
    )(x2d, g, w_main, w_dt)


def _split3(a):
    a1 = a.astype(BF16)
    r1 = a - a1.astype(F32)
    a2 = r1.astype(BF16)
    a3 = (r1 - a2.astype(F32)).astype(BF16)
    return a1, a2, a3


def _ssd_kernel(z_ref, x_ref, b_ref, c_ref, dtc_ref, dtr_ref, cw_ref, cb_ref, pc_ref, pr_ref, dexp_ref,
                gn_ref, tril_ref, expand_ref, shift_ref, o_ref, state_ref, tail_ref):
    L = SSD_CHUNK

    @pl.when(pl.program_id(2) == 0)
    def _():
        state_ref[...] = jnp.zeros_like(state_ref)
        tail_ref[...] = jnp.zeros_like(tail_ref)

    raw_b = jnp.concatenate([x_ref[...], b_ref[...], c_ref[...]], axis=1)
    raw = raw_b.astype(F32)
    shifted = _dot(shift_ref[...], raw_b)
    edge = jnp.concatenate([tail_ref[...], raw[:HALO, :]], axis=0)
    tail_ref[...] = raw[L - HALO:, :]
    cw = cw_ref[0]
    acc = cb_ref[0] + cw[CONV_K - 1:CONV_K, :] * raw
    for k in range(CONV_K - 1):
        d = CONV_K - 1 - k
        tap = jnp.concatenate([edge[HALO - d:2 * HALO - d, :], shifted[(d - 1) * L + HALO:d * L, :]], axis=0)
        acc = acc + cw[k:k + 1, :] * tap
    u = _silu(acc)
    xs = u[:, :GROUP_WIDTH]
    bm = u[:, GROUP_WIDTH:GROUP_WIDTH + SSM_STATE].astype(BF16)
    cm = u[:, GROUP_WIDTH + SSM_STATE:].astype(BF16)

    pc = pc_ref[0]
    pr = pr_ref[0]
    dtc = _softplus(dtc_ref[0, 0] + pc[0:1, :])
    dtr = _softplus(dtr_ref[0, 0] + pr[:, 0:1])
    a_c = dtc * (-jnp.exp(pc[1:2, :]))
    a_r = dtr * (-jnp.exp(pr[:, 1:2]))
    tril = tril_ref[...]
    cum_c = sum(_dot(tril, piece) for piece in _split3(a_c * LOG2E))
    cum_r = sum(_dot_nt(piece, tril) for piece in _split3(a_r * LOG2E))
    e_c = jnp.exp2(cum_c)
    dend_c = jnp.exp2(cum_c[L - 1:L, :] - cum_c)

    expand = expand_ref[...]
    dt_x = _dot(dtc.astype(BF16), expand)
    e_x = _dot(e_c.astype(BF16), expand)
    dend_x = _dot(dend_c.astype(BF16), expand)
    e_last_x = sum(_dot(piece, expand) for piece in _split3(e_c[L - HALO:, :]))[HALO - 1:HALO, :]
    lane = lax.broadcasted_iota(jnp.int32, (L, LANES), 1)

    xdt = xs * dt_x
    xdt_b = xdt.astype(BF16)

    cbm = _dot_nt(cm, bm)
    li = lax.broadcasted_iota(jnp.int32, (L, L), 0)
    si = lax.broadcasted_iota(jnp.int32, (L, L), 1)
    causal = si <= li
    y_parts = []
    for p in range(HEADS_PER_GROUP // 2):
        xp = xdt_b[:, p * LANES:(p + 1) * LANES]
        ys = []
        for r in (2 * p, 2 * p + 1):
            seg = cum_c[:, r:r + 1] - cum_r[r:r + 1, :]
            m = (cbm * jnp.exp2(jnp.where(causal, seg, -jnp.inf))).astype(BF16)
            ys.append(_dot(m, xp))
        y_parts.append(jnp.where(lane < SSM_HEAD_DIM, ys[0], ys[1]))
    y = jnp.concatenate(y_parts, axis=1)

    state = state_ref[...]
    y = y + _dot(cm, state.astype(BF16)) * e_x
    state_ref[...] = state * e_last_x + _dot_tn(bm, (xdt * dend_x).astype(BF16))

    y = y + dexp_ref[0] * xs
    y = y * _silu(z_ref[...].astype(F32))
    o_ref[...] = (y * _rms_scale(y) * gn_ref[0]).astype(o_ref.dtype)


def _ssd(proj, dtc, dtr, cw, cb, pc, pr, dexp, gn, tril, expand, shift, *, batch, seq):
    L = SSD_CHUNK
    nc = seq // L
    x_block0 = D_INNER // GROUP_WIDTH
    b_block0 = 2 * D_INNER // SSM_STATE
    c_block0 = b_block0 + SSM_GROUPS
    return pl.pallas_call(
        _ssd_kernel,
        grid=(batch, SSM_GROUPS, nc),
        in_specs=[
            pl.BlockSpec((L, GROUP_WIDTH), lambda b, g, c: (b * nc + c, g)),
            pl.BlockSpec((L, GROUP_WIDTH), lambda b, g, c: (b * nc + c, x_block0 + g)),
            pl.BlockSpec((L, SSM_STATE), lambda b, g, c: (b * nc + c, b_block0 + g)),
            pl.BlockSpec((L, SSM_STATE), lambda b, g, c: (b * nc + c, c_block0 + g)),
            pl.BlockSpec((1, 1, L, HEADS_PER_GROUP), lambda b, g, c: (b, g, c, 0)),
            pl.BlockSpec((1, 1, HEADS_PER_GROUP, L), lambda b, g, c: (b, g, 0, c)),
            pl.BlockSpec((1, CONV_K, XBC_WIDTH), lambda b, g, c: (g, 0, 0)),
            pl.BlockSpec((1, 1, XBC_WIDTH), lambda b, g, c: (g, 0, 0)),
            pl.BlockSpec((1, 2, HEADS_PER_GROUP), lambda b, g, c: (g, 0, 0)),
            pl.BlockSpec((1, HEADS_PER_GROUP, 2), lambda b, g, c: (g, 0, 0)),
            pl.BlockSpec((1, 1, GROUP_WIDTH), lambda b, g, c: (g, 0, 0)),
            pl.BlockSpec((1, 1, GROUP_WIDTH), lambda b, g, c: (g, 0, 0)),
            pl.BlockSpec((L, L), lambda b, g, c: (0, 0)),
            pl.BlockSpec((HEADS_PER_GROUP, GROUP_WIDTH), lambda b, g, c: (0, 0)),
            pl.BlockSpec(((CONV_K - 1) * L, L), lambda b, g, c: (0, 0)),
        ],
        out_specs=pl.BlockSpec((L, GROUP_WIDTH), lambda b, g, c: (b * nc + c, g)),
        out_shape=jax.ShapeDtypeStruct((batch * seq, D_INNER), BF16),
        scratch_shapes=[pltpu.VMEM((SSM_STATE, GROUP_WIDTH), F32),
                        pltpu.VMEM((HALO, XBC_WIDTH), F32)],
        compiler_params=_cparams("parallel", "parallel", "arbitrary"),
        name="ssd",
    )(proj, proj, proj, proj, dtc, dtr, cw, cb, pc, pr, dexp, gn, tril, expand, shift)


def _proj_res_kernel(y_ref, w_ref, r_ref, x1_ref, xn_ref):
    x1 = r_ref[...] + _dot(y_ref[...], w_ref[...])
    x1_ref[...] = x1
    xn_ref[...] = (x1 * _rms_scale(x1)).astype(xn_ref.dtype)


def _proj_res(y, w, res, *, tm=512):
    t, k = y.shape
    n = w.shape[1]
    return pl.pallas_call(
        _proj_res_kernel,
        grid=(t // tm,),
        in_specs=[
            pl.BlockSpec((tm, k), lambda i: (i, 0)),
            pl.BlockSpec((k, n), lambda i: (0, 0), pipeline_mode=pl.Buffered(1)),
            pl.BlockSpec((tm, n), lambda i: (i, 0)),
        ],
        out_specs=[
            pl.BlockSpec((tm, n), lambda i: (i, 0)),
            pl.BlockSpec((tm, n), lambda i: (i, 0)),
        ],
        out_shape=[
            jax.ShapeDtypeStruct((t, n), F32),
            jax.ShapeDtypeStruct((t, n), BF16),
        ],
        compiler_params=_cparams("parallel"),
        name="a_out_proj",
    )(y, w, res)


def _b_proj_kernel(xn_ref, w_ref, cos_ref, sin_ref, o_ref, *, tiles_per_section):
    tm, tn = o_ref.shape
    section = pl.program_id(1) // tiles_per_section
    acc = _dot(xn_ref[...], w_ref[...])

    is_rope = section % 2 == 0
    scale = jnp.where(section == 2, QK_SCALE_LOG2E, 1.0)
    cos = jnp.where(is_rope, cos_ref[...] * scale, 1.0)
    sin = jnp.where(is_rope, sin_ref[...] * scale, 0.0)
    for h in range(tn // ATT_HEAD_DIM):
        sl = slice(h * ATT_HEAD_DIM, (h + 1) * ATT_HEAD_DIM)
        a = acc[:, sl]
        o_ref[:, sl] = (a * cos + pltpu.roll(a, ATT_HEAD_DIM // 2, axis=1) * sin).astype(o_ref.dtype)


def _b_proj(xn, w, cos, sin, *, seq, tm=1024, tn=2048):
    t, d = xn.shape
    n = w.shape[1]
    tiles_per_section = ATT_WIDTH // tn
    s_tiles = seq // tm
    return pl.pallas_call(
        functools.partial(_b_proj_kernel, tiles_per_section=tiles_per_section),
        grid=(t // tm, n // tn),
        in_specs=[
            pl.BlockSpec((tm, d), lambda i, j: (i, 0)),
            pl.BlockSpec((d, tn), lambda i, j: (0, j)),
            pl.BlockSpec((tm, ATT_HEAD_DIM), lambda i, j: (i % s_tiles, 0)),
            pl.BlockSpec((tm, ATT_HEAD_DIM), lambda i, j: (i % s_tiles, 0)),
        ],
        out_specs=pl.BlockSpec((tm, tn), lambda i, j: (i, j)),
        out_shape=jax.ShapeDtypeStruct((t, n), BF16),
        compiler_params=_cparams("parallel", "parallel"),
        name="b_proj",
    )(xn, w, cos, sin)


def _moba_kernel(q_ref, k_ref, v_ref, z_ref, o_ref, kbar_ref, off_ref, acc_ref, sa_ref, sb_ref, *, heads):
    qi = pl.program_id(2)
    blk = MOBA_BLOCK
    hd = ATT_HEAD_DIM
    nb = kbar_ref.shape[0]
    cols = [slice(h * hd, (h + 1) * hd) for h in range(heads)]
    qs = [q_ref[:, cs] for cs in cols]

    @pl.when(qi == 0)
    def _():
        def block_mean(n, carry):
            r = pl.multiple_of(n * blk, blk)
            kbar_ref[pl.ds(n, 1), :] = jnp.mean(k_ref[pl.ds(r, blk), :].astype(F32), axis=0, keepdims=True)
            return carry
        lax.fori_loop(0, nb, block_mean, 0)

    row0 = pl.multiple_of(qi * blk, blk)
    bidx = lax.broadcasted_iota(jnp.int32, (nb, blk), 0)
    kpos = lax.broadcasted_iota(jnp.int32, (blk, blk), 0)
    qpos = lax.broadcasted_iota(jnp.int32, (blk, blk), 1)
    m_init, l_init, off_init, acc_init = [], [], [], []
    for h in range(heads):
        gate = _dot_nt(kbar_ref[:, cols[h]].astype(BF16), qs[h])
        gate = jnp.where(bidx < qi, gate, -jnp.inf)
        sel = jnp.zeros((nb, blk), dtype=jnp.bool_)
        for _ in range(MOBA_TOPK):
            best = jnp.max(gate, axis=0, keepdims=True)
            first = jnp.min(jnp.where(gate == best, bidx, nb), axis=0, keepdims=True)
            pick = (bidx == first) & (best > -jnp.inf)
            sel = sel | pick
            gate = jnp.where(pick, -jnp.inf, gate)
        off_init.append(jnp.where(sel, 0.0, jnp.inf))

        s = _dot_nt(k_ref[pl.ds(row0, blk), cols[h]], qs[h])
        s = jnp.where(kpos <= qpos, s, -jnp.inf)
        m0 = jnp.max(s, axis=0, keepdims=True)
        p = jnp.exp2(s - m0)
        m_init.append(m0)
        l_init.append(jnp.sum(p, axis=0, keepdims=True))
        acc_init.append(_dot_tn(v_ref[pl.ds(row0, blk), cols[h]], p.astype(BF16)))
    for h in range(heads):
        off_ref[h] = off_init[h]
        acc_ref[h] = acc_init[h]

    def issue_scores(j, dst_ref):
        r = pl.multiple_of(j * blk, blk)
        for h in range(heads):
            dst_ref[h] = _dot_nt(k_ref[pl.ds(r, blk), cols[h]], qs[h])

    def half_step(j, ms, ls, cur_ref, nxt_ref):
        jc = jnp.minimum(j, qi - 1)
        r0 = pl.multiple_of(jc * blk, blk)
        live = j < qi
        offs = [jnp.where(live, off_ref[h, pl.ds(jc, 1), :], jnp.inf) for h in range(heads)]
        issue_scores(jnp.minimum(j + 1, qi - 1), nxt_ref)
        new_ms, new_ls, alphas, pvs = [], [], [], []
        for h in range(heads):
            sj = cur_ref[h]
            m_new = jnp.maximum(ms[h], jnp.max(sj, axis=0, keepdims=True) - offs[h])
            alpha = jnp.exp2(ms[h] - m_new)
            pj = jnp.exp2(sj - (m_new + offs[h]))
            new_ms.append(m_new)
            new_ls.append(alpha * ls[h] + jnp.sum(pj, axis=0, keepdims=True))
            alphas.append(alpha)
            pvs.append(_dot_tn(v_ref[pl.ds(r0, blk), cols[h]], pj.astype(BF16)))
        new_accs = [alphas[h] * acc_ref[h] + pvs[h] for h in range(heads)]
        for h in range(heads):
            acc_ref[h] = new_accs[h]
        return tuple(new_ms), tuple(new_ls)

    def body(jj, carry):
        ms, ls = carry
        ms, ls = half_step(2 * jj, ms, ls, sa_ref, sb_ref)
        return half_step(2 * jj + 1, ms, ls, sb_ref, sa_ref)

    @pl.when(qi > 0)
    def _():
        issue_scores(0, sa_ref)

    _, ls = lax.fori_loop(0, (qi + 1) // 2, body, (tuple(m_init), tuple(l_init)))
    outs = [(acc_ref[h] * (1.0 / ls[h])).T for h in range(heads)]
    gates = _silu(z_ref[...].astype(F32))
    o_ref[...] = (jnp.concatenate(outs, axis=1) * gates).astype(o_ref.dtype)


def _moba(kvqz, *, batch, seq, heads=4):
    blk = MOBA_BLOCK
    nb = seq // blk
    w = heads * ATT_HEAD_DIM
    ngroups = ATT_HEADS // heads
    return pl.pallas_call(
        functools.partial(_moba_kernel, heads=heads),
        grid=(batch, ngroups, nb),
        in_specs=[
            pl.BlockSpec((blk, w), lambda b, h, i: (b * nb + i, 2 * ngroups + h)),
            pl.BlockSpec((seq, w), lambda b, h, i: (b, h)),
            pl.BlockSpec((seq, w), lambda b, h, i: (b, ngroups + h)),
            pl.BlockSpec((blk, w), lambda b, h, i: (b * nb + i, 3 * ngroups + h)),
        ],
        out_specs=pl.BlockSpec((blk, w), lambda b, h, i: (b * nb + i, h)),
        out_shape=jax.ShapeDtypeStruct((batch * seq, ATT_WIDTH), BF16),
        scratch_shapes=[pltpu.VMEM((nb, w), F32),
                        pltpu.VMEM((heads, nb, blk), F32),
                        pltpu.VMEM((heads, ATT_HEAD_DIM, blk), F32),
                        pltpu.VMEM((heads, blk, blk), F32),
                        pltpu.VMEM((heads, blk, blk), F32)],
        compiler_params=_cparams("parallel", "parallel", "arbitrary"),
        name="moba",
    )(kvqz, kvqz, kvqz, kvqz)


def _final_kernel(o_ref, w_ref, r_ref, g_ref, out_ref):
    x = r_ref[...] + _dot(o_ref[...], w_ref[...])
    out_ref[...] = x * _rms_scale(x) * g_ref[...]


def _final(o, w, res, g, *, tm=512):
    t, k = o.shape
    n = w.shape[1]
    return pl.pallas_call(
        _final_kernel,
        grid=(t // tm,),
        in_specs=[
            pl.BlockSpec((tm, k), lambda i: (i, 0)),
            pl.BlockSpec((k, n), lambda i: (0, 0)),
            pl.BlockSpec((tm, n), lambda i: (i, 0)),
            pl.BlockSpec((1, n), lambda i: (0, 0)),
        ],
        out_specs=pl.BlockSpec((tm, n), lambda i: (i, 0)),
        out_shape=jax.ShapeDtypeStruct((t, n), F32),
        compiler_params=_cparams("parallel"),
        name="final",
    )(o, w, res, g)


def _rope_tables(seq):
    hd = ATT_HEAD_DIM
    inv_freq = ROPE_THETA ** (-jnp.arange(0, hd, 2, dtype=F32) / hd)
    ang = jnp.arange(seq, dtype=F32)[:, None] * inv_freq[None, :]
    cos, sin = jnp.cos(ang), jnp.sin(ang)
    return jnp.concatenate([cos, cos], axis=1), jnp.concatenate([-sin, sin], axis=1)


def kernel(x, a_norm_g, a_w_in, a_conv_w, a_conv_b, a_dt_bias, a_A_log, a_D, a_gnorm_g, a_w_out,
           kv_norm_g, w_kv, b_norm_g, b_w_in, b_w_out, final_norm_g):
    batch, seq, d = x.shape
    assert d == D_MODEL and seq % SSD_CHUNK == 0 and seq % MOBA_BLOCK == 0
    assert a_w_in.shape[0] == 1 and b_w_in.shape[0] == 1
    t = batch * seq
    G, R, N = SSM_GROUPS, HEADS_PER_GROUP, SSM_STATE
    x2d = x.reshape(t, d)

    w_in = a_w_in[0]
    n_main = 2 * D_INNER + 2 * G * N
    w_main = w_in[:, :n_main].astype(BF16)
    w_dt = jnp.pad(w_in[:, n_main:].astype(BF16), ((0, 0), (0, LANES - SSM_HEADS)))

    def group_major(p):
        rows = p.shape[0]
        px = p[:, :D_INNER].reshape(rows, G, GROUP_WIDTH)
        pb = p[:, D_INNER:D_INNER + G * N].reshape(rows, G, N)
        pcc = p[:, D_INNER + G * N:].reshape(rows, G, N)
        return jnp.transpose(jnp.concatenate([px, pb, pcc], axis=2), (1, 0, 2))

    cw = group_major(a_conv_w[0])
    cb = group_major(a_conv_b[0][None, :])
    dt_bias = a_dt_bias[0].reshape(G, R)
    a_log = a_A_log[0].reshape(G, R)
    pc = jnp.stack([dt_bias, a_log], axis=1)
    pr = jnp.stack([dt_bias, a_log], axis=2)
    dexp = jnp.repeat(a_D[0], SSM_HEAD_DIM).reshape(G, 1, GROUP_WIDTH)
    gn = a_gnorm_g[0].reshape(G, 1, GROUP_WIDTH)
    tril = jnp.tril(jnp.ones((SSD_CHUNK, SSD_CHUNK), BF16))
    expand = jnp.repeat(jnp.eye(R, dtype=BF16), SSM_HEAD_DIM, axis=1)
    shift = jnp.concatenate([jnp.eye(SSD_CHUNK, k=-dd, dtype=BF16) for dd in range(1, CONV_K)], axis=0)

    proj, dt_raw = _a_in_proj(x2d, a_norm_g[0][None, :], w_main, w_dt)
    dt4 = dt_raw[:, :SSM_HEADS].reshape(batch, seq, G, R)
    dtc = jnp.transpose(dt4, (0, 2, 1, 3))
    dtr = jnp.transpose(dt4, (0, 2, 3, 1))
    y = _ssd(proj, dtc, dtr, cw, cb, pc, pr, dexp, gn, tril, expand, shift, batch=batch, seq=seq)
    x1, xn = _proj_res(y, a_w_out[0].astype(BF16), x2d)

    cos, sin = _rope_tables(seq)
    w_b = jnp.concatenate([(kv_norm_g[:, None] * w_kv).astype(BF16),
                           (b_norm_g[0][:, None] * b_w_in[0]).astype(BF16)], axis=1)
    kvqz = _b_proj(xn, w_b, cos, sin, seq=seq)
    o = _moba(kvqz, batch=batch, seq=seq)
    out = _final(o, b_w_out[0].astype(BF16), x1, final_norm_g[None, :])
    return out.reshape(batch, seq, d)
```

```python
import functools

import jax
import jax.numpy as jnp
from jax import lax
from jax.experimental import pallas as pl
from jax.experimental.pallas import tpu as pltpu

F32 = jnp.float32
BF16 = jnp.bfloat16

D_MODEL = 2048
SSM_HEAD_DIM = 64
SSM_GROUPS = 8
SSM_STATE = 128
CONV_K = 4
SSD_CHUNK = 256
ATT_HEADS = 16
ATT_HEAD_DIM = 128
MOBA_BLOCK = 256
MOBA_TOPK = 3
ROPE_THETA = 10000.0
EPS = 1e-5

D_INNER = 2 * D_MODEL
SSM_HEADS = D_INNER // SSM_HEAD_DIM
HEADS_PER_GROUP = SSM_HEADS // SSM_GROUPS
GROUP_WIDTH = HEADS_PER_GROUP * SSM_HEAD_DIM
XBC_WIDTH = GROUP_WIDTH + 2 * SSM_STATE
ATT_WIDTH = ATT_HEADS * ATT_HEAD_DIM
LOG2E = 1.4426950408889634
QK_SCALE_LOG2E = (ATT_HEAD_DIM ** -0.5) * LOG2E
HALO = 8

LANES = 128
VMEM_LIMIT = 60 * 1024 * 1024


def _cparams(*sem):
    return pltpu.CompilerParams(dimension_semantics=sem, vmem_limit_bytes=VMEM_LIMIT)


def _rms_scale(x):
    return lax.rsqrt(jnp.mean(x * x, axis=-1, keepdims=True) + EPS)


def _silu(x):
    return x * jax.nn.sigmoid(x)


def _softplus(x):
    return jnp.maximum(x, 0.0) + jnp.log1p(jnp.exp(-jnp.abs(x)))


def _dot(a, b):
    return jnp.dot(a, b, preferred_element_type=F32)


def _dot_nt(a, b):
    return lax.dot_general(a, b, (((1,), (1,)), ((), ())), preferred_element_type=F32)


def _dot_tn(a, b):
    return lax.dot_general(a, b, (((0,), (0,)), ((), ())), preferred_element_type=F32)


def _a_in_proj_kernel(x_ref, g_ref, w_ref, wdt_ref, o_ref, dtt_ref, hn_ref):
    @pl.when(pl.program_id(1) == 0)
    def _():
        x = x_ref[...]
        hn = (x * _rms_scale(x) * g_ref[...]).astype(BF16)
        hn_ref[...] = hn
        dtt_ref[...] = _dot_nt(wdt_ref[...], hn)

    o_ref[...] = _dot(hn_ref[...], w_ref[...]).astype(o_ref.dtype)


def _a_in_proj(x2d, g, w_main, w_dt_t, *, tm=1024, tn=2048):
    t, d = x2d.shape
    n = w_main.shape[1]
    nh = w_dt_t.shape[0]
    return pl.pallas_call(
        _a_in_proj_kernel,
        grid=(t // tm, n // tn),
        in_specs=[
            pl.BlockSpec((tm, d), lambda i, j: (i, 0)),
            pl.BlockSpec((1, d), lambda i, j: (0, 0)),
            pl.BlockSpec((d, tn), lambda i, j: (0, j)),
            pl.BlockSpec((nh, d), lambda i, j: (0, 0)),
        ],
        out_specs=[
            pl.BlockSpec((tm, tn), lambda i, j: (i, j)),
            pl.BlockSpec((nh, tm), lambda i, j: (0, i)),
        ],
        out_shape=[
            jax.ShapeDtypeStruct((t, n), BF16),
            jax.ShapeDtypeStruct((nh, t), F32),
        ],
        scratch_shapes=[pltpu.VMEM((tm, d), BF16)],
        compiler_params=_cparams("parallel", "arbitrary"),
        name="a_in_proj",
    )(x2d, g, w_main, w_dt_t)


def _split3(a):
    a1 = a.astype(BF16)
    r1 = a - a1.astype(F32)
    a2 = r1.astype(BF16)
    a3 = (r1 - a2.astype(F32)).astype(BF16)
    return a1, a2, a3


def _ssd_group(z_b, x_b, b_b, c_b, dt_raw, cw, cb, pr, dexp, gn, tril, expand, shift, state, tail):
    L = SSD_CHUNK

    raw_b = jnp.concatenate([x_b, b_b, c_b], axis=1)
    raw = raw_b.astype(F32)
    shifted = _dot(shift, raw_b)
    edge = jnp.concatenate([tail, raw[:HALO, :]], axis=0)
    acc = cb + cw[CONV_K - 1:CONV_K, :] * raw
    for k in range(CONV_K - 1):
        d = CONV_K - 1 - k
        tap = jnp.concatenate([edge[HALO - d:2 * HALO - d, :], shifted[(d - 1) * L + HALO:d * L, :]], axis=0)
        acc = acc + cw[k:k + 1, :] * tap
    u = _silu(acc)
    xs = u[:, :GROUP_WIDTH]
    bm = u[:, GROUP_WIDTH:GROUP_WIDTH + SSM_STATE].astype(BF16)
    cm = u[:, GROUP_WIDTH + SSM_STATE:].astype(BF16)

    dtr = _softplus(dt_raw + pr[:, 0:1])
    a_pieces = _split3(dtr * (-jnp.exp(pr[:, 1:2])) * LOG2E)
    cum_r = sum(_dot_nt(piece, tril) for piece in a_pieces)
    cum_c = sum(_dot_nt(tril, piece) for piece in a_pieces)
    e_c = jnp.exp2(cum_c)
    dend_c = jnp.exp2(cum_c[L - 1:L, :] - cum_c)

    dt_x = _dot_tn(dtr.astype(BF16), expand)
    e_x = _dot(e_c.astype(BF16), expand)
    dend_x = _dot(dend_c.astype(BF16), expand)
    e_last_x = sum(_dot(piece, expand) for piece in _split3(e_c[L - HALO:, :]))[HALO - 1:HALO, :]
    lane = lax.broadcasted_iota(jnp.int32, (L, LANES), 1)

    xdt = xs * dt_x
    xdt_b = xdt.astype(BF16)

    cbm = _dot_nt(cm, bm)
    li = lax.broadcasted_iota(jnp.int32, (L, L), 0)
    si = lax.broadcasted_iota(jnp.int32, (L, L), 1)
    causal = si <= li
    y_parts = []
    for p in range(HEADS_PER_GROUP // 2):
        xp = xdt_b[:, p * LANES:(p + 1) * LANES]
        ys = []
        for r in (2 * p, 2 * p + 1):
            seg = cum_c[:, r:r + 1] - cum_r[r:r + 1, :]
            m = (cbm * jnp.exp2(jnp.where(causal, seg, -jnp.inf))).astype(BF16)
            ys.append(_dot(m, xp))
        y_parts.append(jnp.where(lane < SSM_HEAD_DIM, ys[0], ys[1]))
    y = jnp.concatenate(y_parts, axis=1)

    y = y + _dot(cm, state.astype(BF16)) * e_x
    new_state = state * e_last_x + _dot_tn(bm, (xdt * dend_x).astype(BF16))

    y = y + dexp * xs
    y = y * _silu(z_b.astype(F32))
    return y * _rms_scale(y) * gn, new_state, raw[L - HALO:, :]


def _ssd_kernel(z_ref, x_ref, b_ref, c_ref, dtr_ref, cw_ref, cb_ref, pr_ref, dexp_ref,
                gn_ref, tril_ref, expand_ref, shift_ref, o_ref, state_ref, tail_ref, *, groups):
    @pl.when(pl.program_id(2) == 0)
    def _():
        state_ref[...] = jnp.zeros_like(state_ref)
        tail_ref[...] = jnp.zeros_like(tail_ref)

    tril = tril_ref[...]
    expand = expand_ref[...]
    shift = shift_ref[...]
    results = []
    for i in range(groups):
        wide = slice(i * GROUP_WIDTH, (i + 1) * GROUP_WIDTH)
        narrow = slice(i * SSM_STATE, (i + 1) * SSM_STATE)
        heads = slice(i * HEADS_PER_GROUP, (i + 1) * HEADS_PER_GROUP)
        results.append(_ssd_group(
            z_ref[:, wide], x_ref[:, wide], b_ref[:, narrow], c_ref[:, narrow], dtr_ref[heads, :],
            cw_ref[i], cb_ref[i], pr_ref[i], dexp_ref[i], gn_ref[i], tril, expand, shift,
            state_ref[i], tail_ref[i]))
    for i, (y, new_state, new_tail) in enumerate(results):
        o_ref[:, i * GROUP_WIDTH:(i + 1) * GROUP_WIDTH] = y.astype(o_ref.dtype)
        state_ref[i] = new_state
        tail_ref[i] = new_tail


def _ssd(proj, dt_t, cw, cb, pr, dexp, gn, tril, expand, shift, *, batch, seq, groups=8):
    L = SSD_CHUNK
    nc = seq // L
    x_block0 = D_INNER // (groups * GROUP_WIDTH)
    b_block0 = 2 * D_INNER // (groups * SSM_STATE)
    c_block0 = b_block0 + SSM_GROUPS // groups
    return pl.pallas_call(
        functools.partial(_ssd_kernel, groups=groups),
        grid=(batch, SSM_GROUPS // groups, nc),
        in_specs=[
            pl.BlockSpec((L, groups * GROUP_WIDTH), lambda b, g, c: (b * nc + c, g)),
            pl.BlockSpec((L, groups * GROUP_WIDTH), lambda b, g, c: (b * nc + c, x_block0 + g)),
            pl.BlockSpec((L, groups * SSM_STATE), lambda b, g, c: (b * nc + c, b_block0 + g)),
            pl.BlockSpec((L, groups * SSM_STATE), lambda b, g, c: (b * nc + c, c_block0 + g)),
            pl.BlockSpec((groups * HEADS_PER_GROUP, L), lambda b, g, c: (g, b * nc + c)),
            pl.BlockSpec((groups, CONV_K, XBC_WIDTH), lambda b, g, c: (g, 0, 0)),
            pl.BlockSpec((groups, 1, XBC_WIDTH), lambda b, g, c: (g, 0, 0)),
            pl.BlockSpec((groups, HEADS_PER_GROUP, 2), lambda b, g, c: (g, 0, 0)),
            pl.BlockSpec((groups, 1, GROUP_WIDTH), lambda b, g, c: (g, 0, 0)),
            pl.BlockSpec((groups, 1, GROUP_WIDTH), lambda b, g, c: (g, 0, 0)),
            pl.BlockSpec((L, L), lambda b, g, c: (0, 0)),
            pl.BlockSpec((HEADS_PER_GROUP, GROUP_WIDTH), lambda b, g, c: (0, 0)),
            pl.BlockSpec(((CONV_K - 1) * L, L), lambda b, g, c: (0, 0)),
        ],
        out_specs=pl.BlockSpec((L, groups * GROUP_WIDTH), lambda b, g, c: (b * nc + c, g)),
        out_shape=jax.ShapeDtypeStruct((batch * seq, D_INNER), BF16),
        scratch_shapes=[pltpu.VMEM((groups, SSM_STATE, GROUP_WIDTH), F32),
                        pltpu.VMEM((groups, HALO, XBC_WIDTH), F32)],
        compiler_params=_cparams("parallel", "parallel", "arbitrary"),
        name="ssd",
    )(proj, proj, proj, proj, dt_t, cw, cb, pr, dexp, gn, tril, expand, shift)


def _proj_res_kernel(y_ref, w_ref, r_ref, x1_ref, xn_ref):
    x1 = r_ref[...] + _dot(y_ref[...], w_ref[...])
    x1_ref[...] = x1
    xn_ref[...] = (x1 * _rms_scale(x1)).astype(xn_ref.dtype)


def _proj_res(y, w, res, *, tm=512):
    t, k = y.shape
    n = w.shape[1]
    return pl.pallas_call(
        _proj_res_kernel,
        grid=(t // tm,),
        in_specs=[
            pl.BlockSpec((tm, k), lambda i: (i, 0)),
            pl.BlockSpec((k, n), lambda i: (0, 0), pipeline_mode=pl.Buffered(1)),
            pl.BlockSpec((tm, n), lambda i: (i, 0)),
        ],
        out_specs=[
            pl.BlockSpec((tm, n), lambda i: (i, 0)),
            pl.BlockSpec((tm, n), lambda i: (i, 0)),
        ],
        out_shape=[
            jax.ShapeDtypeStruct((t, n), F32),
            jax.ShapeDtypeStruct((t, n), BF16),
        ],
        compiler_params=_cparams("parallel"),
        name="a_out_proj",
    )(y, w, res)


def _b_proj_kernel(xn_ref, w_ref, cos_ref, sin_ref, o_ref, *, tiles_per_section):
    tm, tn = o_ref.shape
    section = pl.program_id(1) // tiles_per_section
    acc = _dot(xn_ref[...], w_ref[...])

    is_rope = section % 2 == 0
    scale = jnp.where(section == 2, QK_SCALE_LOG2E, 1.0)
    cos = jnp.where(is_rope, cos_ref[...] * scale, 1.0)
    sin = jnp.where(is_rope, sin_ref[...] * scale, 0.0)
    for h in range(tn // ATT_HEAD_DIM):
        sl = slice(h * ATT_HEAD_DIM, (h + 1) * ATT_HEAD_DIM)
        a = acc[:, sl]
        o_ref[:, sl] = (a * cos + pltpu.roll(a, ATT_HEAD_DIM // 2, axis=1) * sin).astype(o_ref.dtype)


def _b_proj(xn, w, cos, sin, *, seq, tm=1024, tn=2048):
    t, d = xn.shape
    n = w.shape[1]
    tiles_per_section = ATT_WIDTH // tn
    s_tiles = seq // tm
    return pl.pallas_call(
        functools.partial(_b_proj_kernel, tiles_per_section=tiles_per_section),
        grid=(t // tm, n // tn),
        in_specs=[
            pl.BlockSpec((tm, d), lambda i, j: (i, 0)),
            pl.BlockSpec((d, tn), lambda i, j: (0, j)),
            pl.BlockSpec((tm, ATT_HEAD_DIM), lambda i, j: (i % s_tiles, 0)),
            pl.BlockSpec((tm, ATT_HEAD_DIM), lambda i, j: (i % s_tiles, 0)),
        ],
        out_specs=pl.BlockSpec((tm, tn), lambda i, j: (i, j)),
        out_shape=jax.ShapeDtypeStruct((t, n), BF16),
        compiler_params=_cparams("parallel", "parallel"),
        name="b_proj",
    )(xn, w, cos, sin)


def _moba_kernel(q_ref, k_ref, v_ref, z_ref, o_ref, kbar_ref, off_ref, acc_ref, sa_ref, sb_ref, *, heads):
    qi = pl.program_id(2)
    blk = MOBA_BLOCK
    hd = ATT_HEAD_DIM
    nb = kbar_ref.shape[0]
    cols = [slice(h * hd, (h + 1) * hd) for h in range(heads)]
    qs = [q_ref[:, cs] for cs in cols]

    @pl.when(qi == 0)
    def _():
        def block_mean(n, carry):
            r = pl.multiple_of(n * blk, blk)
            kbar_ref[pl.ds(n, 1), :] = jnp.mean(k_ref[pl.ds(r, blk), :].astype(F32), axis=0, keepdims=True)
            return carry
        lax.fori_loop(0, nb, block_mean, 0)

    row0 = pl.multiple_of(qi * blk, blk)
    bidx = lax.broadcasted_iota(jnp.int32, (nb, blk), 0)
    kpos = lax.broadcasted_iota(jnp.int32, (blk, blk), 0)
    qpos = lax.broadcasted_iota(jnp.int32, (blk, blk), 1)
    m_init, l_init, off_init, acc_init = [], [], [], []
    for h in range(heads):
        gate = _dot_nt(kbar_ref[:, cols[h]].astype(BF16), qs[h])
        gate = jnp.where(bidx < qi, gate, -jnp.inf)
        sel = jnp.zeros((nb, blk), dtype=jnp.bool_)
        for _ in range(MOBA_TOPK):
            best = jnp.max(gate, axis=0, keepdims=True)
            first = jnp.min(jnp.where(gate == best, bidx, nb), axis=0, keepdims=True)
            pick = (bidx == first) & (best > -jnp.inf)
            sel = sel | pick
            gate = jnp.where(pick, -jnp.inf, gate)
        off_init.append(jnp.where(sel, 0.0, jnp.inf))

        s = _dot_nt(k_ref[pl.ds(row0, blk), cols[h]], qs[h])
        s = jnp.where(kpos <= qpos, s, -jnp.inf)
        m0 = jnp.max(s, axis=0, keepdims=True)
        p = jnp.exp2(s - m0)
        m_init.append(m0)
        l_init.append(jnp.sum(p, axis=0, keepdims=True))
        acc_init.append(_dot_tn(v_ref[pl.ds(row0, blk), cols[h]], p.astype(BF16)))
    for h in range(heads):
        off_ref[h] = off_init[h]
        acc_ref[h] = acc_init[h]

    def issue_scores(j, dst_ref):
        r = pl.multiple_of(j * blk, blk)
        maxima = []
        for h in range(heads):
            s = _dot_nt(k_ref[pl.ds(r, blk), cols[h]], qs[h])
            dst_ref[h] = s
            maxima.append(jnp.max(s, axis=0, keepdims=True))
        return tuple(maxima)

    def half_step(j, ms, ls, cur_max, cur_ref, nxt_ref):
        jc = jnp.minimum(j, qi - 1)
        r0 = pl.multiple_of(jc * blk, blk)
        live = j < qi
        offs = [jnp.where(live, off_ref[h, pl.ds(jc, 1), :], jnp.inf) for h in range(heads)]
        nxt_max = issue_scores(jnp.minimum(j + 1, qi - 1), nxt_ref)
        new_ms, new_ls, alphas, pvs = [], [], [], []
        for h in range(heads):
            sj = cur_ref[h]
            m_new = jnp.maximum(ms[h], cur_max[h] - offs[h])
            alpha = jnp.exp2(ms[h] - m_new)
            pj = jnp.exp2(sj - (m_new + offs[h]))
            new_ms.append(m_new)
            new_ls.append(alpha * ls[h] + jnp.sum(pj, axis=0, keepdims=True))
            alphas.append(alpha)
            pvs.append(_dot_tn(v_ref[pl.ds(r0, blk), cols[h]], pj.astype(BF16)))
        new_accs = [alphas[h] * acc_ref[h] + pvs[h] for h in range(heads)]
        for h in range(heads):
            acc_ref[h] = new_accs[h]
        return tuple(new_ms), tuple(new_ls), nxt_max

    def body(jj, carry):
        ms, ls, max_a = carry
        ms, ls, max_b = half_step(2 * jj, ms, ls, max_a, sa_ref, sb_ref)
        return half_step(2 * jj + 1, ms, ls, max_b, sb_ref, sa_ref)

    max_0 = issue_scores(0, sa_ref)
    _, ls, _ = lax.fori_loop(0, (qi + 1) // 2, body, (tuple(m_init), tuple(l_init), max_0))
    outs = [(acc_ref[h] * (1.0 / ls[h])).T for h in range(heads)]
    gates = _silu(z_ref[...].astype(F32))
    o_ref[...] = (jnp.concatenate(outs, axis=1) * gates).astype(o_ref.dtype)


def _moba(kvqz, *, batch, seq, heads=4):
    blk = MOBA_BLOCK
    nb = seq // blk
    w = heads * ATT_HEAD_DIM
    ngroups = ATT_HEADS // heads
    return pl.pallas_call(
        functools.partial(_moba_kernel, heads=heads),
        grid=(batch, ngroups, nb),
        in_specs=[
            pl.BlockSpec((blk, w), lambda b, h, i: (b * nb + i, 2 * ngroups + h)),
            pl.BlockSpec((seq, w), lambda b, h, i: (b, h)),
            pl.BlockSpec((seq, w), lambda b, h, i: (b, ngroups + h)),
            pl.BlockSpec((blk, w), lambda b, h, i: (b * nb + i, 3 * ngroups + h)),
        ],
        out_specs=pl.BlockSpec((blk, w), lambda b, h, i: (b * nb + i, h)),
        out_shape=jax.ShapeDtypeStruct((batch * seq, ATT_WIDTH), BF16),
        scratch_shapes=[pltpu.VMEM((nb, w), F32),
                        pltpu.VMEM((heads, nb, blk), F32),
                        pltpu.VMEM((heads, ATT_HEAD_DIM, blk), F32),
                        pltpu.VMEM((heads, blk, blk), F32),
                        pltpu.VMEM((heads, blk, blk), F32)],
        compiler_params=_cparams("parallel", "parallel", "arbitrary"),
        name="moba",
    )(kvqz, kvqz, kvqz, kvqz)


def _final_kernel(o_ref, w_ref, r_ref, g_ref, out_ref):
    x = r_ref[...] + _dot(o_ref[...], w_ref[...])
    out_ref[...] = x * _rms_scale(x) * g_ref[...]


def _final(o, w, res, g, *, tm=512):
    t, k = o.shape
    n = w.shape[1]
    return pl.pallas_call(
        _final_kernel,
        grid=(t // tm,),
        in_specs=[
            pl.BlockSpec((tm, k), lambda i: (i, 0)),
            pl.BlockSpec((k, n), lambda i: (0, 0)),
            pl.BlockSpec((tm, n), lambda i: (i, 0)),
            pl.BlockSpec((1, n), lambda i: (0, 0)),
        ],
        out_specs=pl.BlockSpec((tm, n), lambda i: (i, 0)),
        out_shape=jax.ShapeDtypeStruct((t, n), F32),
        compiler_params=_cparams("parallel"),
        name="final",
    )(o, w, res, g)


def _rope_tables(seq):
    hd = ATT_HEAD_DIM
    inv_freq = ROPE_THETA ** (-jnp.arange(0, hd, 2, dtype=F32) / hd)
    ang = jnp.arange(seq, dtype=F32)[:, None] * inv_freq[None, :]
    cos, sin = jnp.cos(ang), jnp.sin(ang)
    return jnp.concatenate([cos, cos], axis=1), jnp.concatenate([-sin, sin], axis=1)


def kernel(x, a_norm_g, a_w_in, a_conv_w, a_conv_b, a_dt_bias, a_A_log, a_D, a_gnorm_g, a_w_out,
           kv_norm_g, w_kv, b_norm_g, b_w_in, b_w_out, final_norm_g):
    batch, seq, d = x.shape
    assert d == D_MODEL and seq % SSD_CHUNK == 0 and seq % MOBA_BLOCK == 0
    assert a_w_in.shape[0] == 1 and b_w_in.shape[0] == 1
    t = batch * seq
    G, R, N = SSM_GROUPS, HEADS_PER_GROUP, SSM_STATE
    x2d = x.reshape(t, d)

    w_in = a_w_in[0]
    n_main = 2 * D_INNER + 2 * G * N
    w_main = w_in[:, :n_main].astype(BF16)
    w_dt_t = w_in[:, n_main:].astype(BF16).T

    def group_major(p):
        rows = p.shape[0]
        px = p[:, :D_INNER].reshape(rows, G, GROUP_WIDTH)
        pb = p[:, D_INNER:D_INNER + G * N].reshape(rows, G, N)
        pcc = p[:, D_INNER + G * N:].reshape(rows, G, N)
        return jnp.transpose(jnp.concatenate([px, pb, pcc], axis=2), (1, 0, 2))

    cw = group_major(a_conv_w[0])
    cb = group_major(a_conv_b[0][None, :])
    dt_bias = a_dt_bias[0].reshape(G, R)
    a_log = a_A_log[0].reshape(G, R)
    pr = jnp.stack([dt_bias, a_log], axis=2)
    dexp = jnp.repeat(a_D[0], SSM_HEAD_DIM).reshape(G, 1, GROUP_WIDTH)
    gn = a_gnorm_g[0].reshape(G, 1, GROUP_WIDTH)
    tril = jnp.tril(jnp.ones((SSD_CHUNK, SSD_CHUNK), BF16))
    expand = jnp.repeat(jnp.eye(R, dtype=BF16), SSM_HEAD_DIM, axis=1)
    shift = jnp.concatenate([jnp.eye(SSD_CHUNK, k=-dd, dtype=BF16) for dd in range(1, CONV_K)], axis=0)

    proj, dt_t = _a_in_proj(x2d, a_norm_g[0][None, :], w_main, w_dt_t)
    y = _ssd(proj, dt_t, cw, cb, pr, dexp, gn, tril, expand, shift, batch=batch, seq=seq)
    x1, xn = _proj_res(y, a_w_out[0].astype(BF16), x2d)

    cos, sin = _rope_tables(seq)
    w_b = jnp.concatenate([(kv_norm_g[:, None] * w_kv).astype(BF16),
                           (b_norm_g[0][:, None] * b_w_in[0]).astype(BF16)], axis=1)
    kvqz = _b_proj(xn, w_b, cos, sin, seq=seq)
    o = _moba(kvqz, batch=batch, seq=seq)
    out = _final(o, b_w_out[0].astype(BF16), x1, final_norm_g[None, :])
    return out.reshape(batch, seq, d)
```

```python
import functools

import jax
import jax.numpy as jnp
from jax import lax
from jax.experimental import pallas as pl
from jax.experimental.pallas import tpu as pltpu

F32 = jnp.float32
BF16 = jnp.bfloat16

D_MODEL = 2048
SSM_HEAD_DIM = 64
SSM_GROUPS = 8
SSM_STATE = 128
CONV_K = 4
SSD_CHUNK = 256
ATT_HEADS = 16
ATT_HEAD_DIM = 128
MOBA_BLOCK = 256
MOBA_TOPK = 3
ROPE_THETA = 10000.0
EPS = 1e-5

D_INNER = 2 * D_MODEL
SSM_HEADS = D_INNER // SSM_HEAD_DIM
HEADS_PER_GROUP = SSM_HEADS // SSM_GROUPS
GROUP_WIDTH = HEADS_PER_GROUP * SSM_HEAD_DIM
XBC_WIDTH = GROUP_WIDTH + 2 * SSM_STATE
ATT_WIDTH = ATT_HEADS * ATT_HEAD_DIM
LOG2E = 1.4426950408889634
QK_SCALE_LOG2E = (ATT_HEAD_DIM ** -0.5) * LOG2E
HALO = 8

LANES = 128
VMEM_LIMIT = 60 * 1024 * 1024


def _cparams(*sem):
    return pltpu.CompilerParams(dimension_semantics=sem, vmem_limit_bytes=VMEM_LIMIT)


def _rms_scale(x):
    return lax.rsqrt(jnp.mean(x * x, axis=-1, keepdims=True) + EPS)


def _silu(x):
    return x * jax.nn.sigmoid(x)


def _softplus(x):
    return jnp.maximum(x, 0.0) + jnp.log1p(jnp.exp(-jnp.abs(x)))


def _dot(a, b):
    return jnp.dot(a, b, preferred_element_type=F32)


def _dot_nt(a, b):
    return lax.dot_general(a, b, (((1,), (1,)), ((), ())), preferred_element_type=F32)


def _dot_tn(a, b):
    return lax.dot_general(a, b, (((0,), (0,)), ((), ())), preferred_element_type=F32)


def _a_in_proj_kernel(x_ref, g_ref, w_ref, wdt_ref, o_ref, dtt_ref, hn_ref):
    @pl.when(pl.program_id(1) == 0)
    def _():
        x = x_ref[...]
        hn = (x * _rms_scale(x) * g_ref[...]).astype(BF16)
        hn_ref[...] = hn
        dtt_ref[...] = _dot_nt(wdt_ref[...], hn)

    o_ref[...] = _dot_nt(hn_ref[...], w_ref[...]).astype(o_ref.dtype)


def _a_in_proj(x2d, g, w_t, w_dt_t, *, n, tm=1024, tn=2048):
    t, d = x2d.shape
    nh = w_dt_t.shape[0]
    return pl.pallas_call(
        _a_in_proj_kernel,
        grid=(t // tm, n // tn),
        in_specs=[
            pl.BlockSpec((tm, d), lambda i, j: (i, 0)),
            pl.BlockSpec((1, d), lambda i, j: (0, 0)),
            pl.BlockSpec((tn, d), lambda i, j: (j, 0)),
            pl.BlockSpec((nh, d), lambda i, j: (0, 0)),
        ],
        out_specs=[
            pl.BlockSpec((tm, tn), lambda i, j: (i, j)),
            pl.BlockSpec((nh, tm), lambda i, j: (0, i)),
        ],
        out_shape=[
            jax.ShapeDtypeStruct((t, n), BF16),
            jax.ShapeDtypeStruct((nh, t), F32),
        ],
        scratch_shapes=[pltpu.VMEM((tm, d), BF16)],
        compiler_params=_cparams("parallel", "arbitrary"),
        name="a_in_proj",
    )(x2d, g, w_t, w_dt_t)


def _split3(a):
    a1 = a.astype(BF16)
    r1 = a - a1.astype(F32)
    a2 = r1.astype(BF16)
    a3 = (r1 - a2.astype(F32)).astype(BF16)
    return a1, a2, a3


def _ssd_group(z_b, x_b, b_b, c_b, dt_raw, cw, cb, pr, dexp, gn, tril, expand, shift, state, tail):
    L = SSD_CHUNK

    raw_b = jnp.concatenate([x_b, b_b, c_b], axis=1)
    raw = raw_b.astype(F32)
    shifted = _dot(shift, raw_b)
    edge = jnp.concatenate([tail, raw[:HALO, :]], axis=0)
    acc = cb + cw[CONV_K - 1:CONV_K, :] * raw
    for k in range(CONV_K - 1):
        d = CONV_K - 1 - k
        tap = jnp.concatenate([edge[HALO - d:2 * HALO - d, :], shifted[(d - 1) * L + HALO:d * L, :]], axis=0)
        acc = acc + cw[k:k + 1, :] * tap
    u = _silu(acc)
    xs = u[:, :GROUP_WIDTH]
    bm = u[:, GROUP_WIDTH:GROUP_WIDTH + SSM_STATE].astype(BF16)
    cm = u[:, GROUP_WIDTH + SSM_STATE:].astype(BF16)

    dtr = _softplus(dt_raw + pr[:, 0:1])
    a_pieces = _split3(dtr * (-jnp.exp(pr[:, 1:2])) * LOG2E)
    cum_r = sum(_dot_nt(piece, tril) for piece in a_pieces)
    cum_c = sum(_dot_nt(tril, piece) for piece in a_pieces)
    e_c = jnp.exp2(cum_c)
    dend_c = jnp.exp2(cum_c[L - 1:L, :] - cum_c)

    dt_x = _dot_tn(dtr.astype(BF16), expand)
    e_x = _dot(e_c.astype(BF16), expand)
    dend_x = _dot(dend_c.astype(BF16), expand)
    e_last_x = sum(_dot(piece, expand) for piece in _split3(e_c[L - HALO:, :]))[HALO - 1:HALO, :]
    lane = lax.broadcasted_iota(jnp.int32, (L, LANES), 1)

    xdt = xs * dt_x
    xdt_b = xdt.astype(BF16)

    cbm = _dot_nt(cm, bm)
    li = lax.broadcasted_iota(jnp.int32, (L, L), 0)
    si = lax.broadcasted_iota(jnp.int32, (L, L), 1)
    causal = si <= li
    y_parts = []
    for p in range(HEADS_PER_GROUP // 2):
        xp = xdt_b[:, p * LANES:(p + 1) * LANES]
        ys = []
        for r in (2 * p, 2 * p + 1):
            seg = cum_c[:, r:r + 1] - cum_r[r:r + 1, :]
            m = (cbm * jnp.exp2(jnp.where(causal, seg, -jnp.inf))).astype(BF16)
            ys.append(_dot(m, xp))
        y_parts.append(jnp.where(lane < SSM_HEAD_DIM, ys[0], ys[1]))
    y = jnp.concatenate(y_parts, axis=1)

    y = y + _dot(cm, state.astype(BF16)) * e_x
    new_state = state * e_last_x + _dot_tn(bm, (xdt * dend_x).astype(BF16))

    y = y + dexp * xs
    y = y * _silu(z_b.astype(F32))
    return y * _rms_scale(y) * gn, new_state, raw[L - HALO:, :]


def _ssd_kernel(z_ref, x_ref, b_ref, c_ref, dtr_ref, cw_ref, cb_ref, pr_ref, dexp_ref,
                gn_ref, tril_ref, expand_ref, shift_ref, o_ref, state_ref, tail_ref, *, groups):
    @pl.when(pl.program_id(2) == 0)
    def _():
        state_ref[...] = jnp.zeros_like(state_ref)
        tail_ref[...] = jnp.zeros_like(tail_ref)

    tril = tril_ref[...]
    expand = expand_ref[...]
    shift = shift_ref[...]
    results = []
    for i in range(groups):
        wide = slice(i * GROUP_WIDTH, (i + 1) * GROUP_WIDTH)
        narrow = slice(i * SSM_STATE, (i + 1) * SSM_STATE)
        heads = slice(i * HEADS_PER_GROUP, (i + 1) * HEADS_PER_GROUP)
        results.append(_ssd_group(
            z_ref[:, wide], x_ref[:, wide], b_ref[:, narrow], c_ref[:, narrow], dtr_ref[heads, :],
            cw_ref[i], cb_ref[i], pr_ref[i], dexp_ref[i], gn_ref[i], tril, expand, shift,
            state_ref[i], tail_ref[i]))
    for i, (y, new_state, new_tail) in enumerate(results):
        o_ref[:, i * GROUP_WIDTH:(i + 1) * GROUP_WIDTH] = y.astype(o_ref.dtype)
        state_ref[i] = new_state
        tail_ref[i] = new_tail


def _ssd(proj, dt_t, cw, cb, pr, dexp, gn, tril, expand, shift, *, batch, seq, groups=8):
    L = SSD_CHUNK
    nc = seq // L
    x_block0 = D_INNER // (groups * GROUP_WIDTH)
    b_block0 = 2 * D_INNER // (groups * SSM_STATE)
    c_block0 = b_block0 + SSM_GROUPS // groups
    return pl.pallas_call(
        functools.partial(_ssd_kernel, groups=groups),
        grid=(batch, SSM_GROUPS // groups, nc),
        in_specs=[
            pl.BlockSpec((L, groups * GROUP_WIDTH), lambda b, g, c: (b * nc + c, g)),
            pl.BlockSpec((L, groups * GROUP_WIDTH), lambda b, g, c: (b * nc + c, x_block0 + g)),
            pl.BlockSpec((L, groups * SSM_STATE), lambda b, g, c: (b * nc + c, b_block0 + g)),
            pl.BlockSpec((L, groups * SSM_STATE), lambda b, g, c: (b * nc + c, c_block0 + g)),
            pl.BlockSpec((groups * HEADS_PER_GROUP, L), lambda b, g, c: (g, b * nc + c)),
            pl.BlockSpec((groups, CONV_K, XBC_WIDTH), lambda b, g, c: (g, 0, 0)),
            pl.BlockSpec((groups, 1, XBC_WIDTH), lambda b, g, c: (g, 0, 0)),
            pl.BlockSpec((groups, HEADS_PER_GROUP, 2), lambda b, g, c: (g, 0, 0)),
            pl.BlockSpec((groups, 1, GROUP_WIDTH), lambda b, g, c: (g, 0, 0)),
            pl.BlockSpec((groups, 1, GROUP_WIDTH), lambda b, g, c: (g, 0, 0)),
            pl.BlockSpec((L, L), lambda b, g, c: (0, 0)),
            pl.BlockSpec((HEADS_PER_GROUP, GROUP_WIDTH), lambda b, g, c: (0, 0)),
            pl.BlockSpec(((CONV_K - 1) * L, L), lambda b, g, c: (0, 0)),
        ],
        out_specs=pl.BlockSpec((L, groups * GROUP_WIDTH), lambda b, g, c: (b * nc + c, g)),
        out_shape=jax.ShapeDtypeStruct((batch * seq, D_INNER), BF16),
        scratch_shapes=[pltpu.VMEM((groups, SSM_STATE, GROUP_WIDTH), F32),
                        pltpu.VMEM((groups, HALO, XBC_WIDTH), F32)],
        compiler_params=_cparams("parallel", "parallel", "arbitrary"),
        name="ssd",
    )(proj, proj, proj, proj, dt_t, cw, cb, pr, dexp, gn, tril, expand, shift)


def _proj_res_kernel(y_ref, w_ref, r_ref, x1_ref, xn_ref):
    x1 = r_ref[...] + _dot(y_ref[...], w_ref[...])
    x1_ref[...] = x1
    xn_ref[...] = (x1 * _rms_scale(x1)).astype(xn_ref.dtype)


def _proj_res(y, w, res, *, tm=512):
    t, k = y.shape
    n = w.shape[1]
    return pl.pallas_call(
        _proj_res_kernel,
        grid=(t // tm,),
        in_specs=[
            pl.BlockSpec((tm, k), lambda i: (i, 0)),
            pl.BlockSpec((k, n), lambda i: (0, 0), pipeline_mode=pl.Buffered(1)),
            pl.BlockSpec((tm, n), lambda i: (i, 0)),
        ],
        out_specs=[
            pl.BlockSpec((tm, n), lambda i: (i, 0)),
            pl.BlockSpec((tm, n), lambda i: (i, 0)),
        ],
        out_shape=[
            jax.ShapeDtypeStruct((t, n), F32),
            jax.ShapeDtypeStruct((t, n), BF16),
        ],
        compiler_params=_cparams("parallel"),
        name="a_out_proj",
    )(y, w, res)


def _b_proj_kernel(xn_ref, wkv_ref, wqz_ref, cos_ref, sin_ref, o_ref, *, tiles_per_section):
    tm, tn = o_ref.shape
    section = pl.program_id(0) // tiles_per_section

    is_rope = section % 2 == 0
    scale = jnp.where(section == 2, QK_SCALE_LOG2E, 1.0)
    cos = jnp.where(is_rope, cos_ref[...] * scale, 1.0)
    sin = jnp.where(is_rope, sin_ref[...] * scale, 0.0)

    def project(w_ref):
        acc = _dot(xn_ref[...], w_ref[...])
        for h in range(tn // ATT_HEAD_DIM):
            sl = slice(h * ATT_HEAD_DIM, (h + 1) * ATT_HEAD_DIM)
            a = acc[:, sl]
            o_ref[:, sl] = (a * cos + pltpu.roll(a, ATT_HEAD_DIM // 2, axis=1) * sin).astype(o_ref.dtype)

    pl.when(section < 2)(functools.partial(project, wkv_ref))
    pl.when(section >= 2)(functools.partial(project, wqz_ref))


def _b_proj(xn, w_kv, w_qz, cos, sin, *, seq, tm=1024, tn=2048):
    t, d = xn.shape
    kv_tiles = w_kv.shape[1] // tn
    n_tiles = kv_tiles + w_qz.shape[1] // tn
    tiles_per_section = ATT_WIDTH // tn
    s_tiles = seq // tm
    return pl.pallas_call(
        functools.partial(_b_proj_kernel, tiles_per_section=tiles_per_section),
        grid=(n_tiles, t // tm),
        in_specs=[
            pl.BlockSpec((tm, d), lambda j, i: (i, 0)),
            pl.BlockSpec((d, tn), lambda j, i: (0, jnp.minimum(j, kv_tiles - 1)),
                         pipeline_mode=pl.Buffered(1)),
            pl.BlockSpec((d, tn), lambda j, i: (0, jnp.maximum(j - kv_tiles, 0)),
                         pipeline_mode=pl.Buffered(1)),
            pl.BlockSpec((tm, ATT_HEAD_DIM), lambda j, i: (i % s_tiles, 0)),
            pl.BlockSpec((tm, ATT_HEAD_DIM), lambda j, i: (i % s_tiles, 0)),
        ],
        out_specs=pl.BlockSpec((tm, tn), lambda j, i: (i, j)),
        out_shape=jax.ShapeDtypeStruct((t, n_tiles * tn), BF16),
        compiler_params=_cparams("parallel", "parallel"),
        name="b_proj",
    )(xn, w_kv, w_qz, cos, sin)


def _moba_kernel(q_ref, k_ref, v_ref, z_ref, o_ref, kbar_ref, off_ref, acc_ref, sa_ref, sb_ref, *, heads):
    qi = pl.program_id(2)
    blk = MOBA_BLOCK
    hd = ATT_HEAD_DIM
    nb = kbar_ref.shape[0]
    cols = [slice(h * hd, (h + 1) * hd) for h in range(heads)]
    qs = [q_ref[:, cs] for cs in cols]

    @pl.when(qi == 0)
    def _():
        def block_mean(n, carry):
            r = pl.multiple_of(n * blk, blk)
            kbar_ref[pl.ds(n, 1), :] = jnp.mean(k_ref[pl.ds(r, blk), :].astype(F32), axis=0, keepdims=True)
            return carry
        lax.fori_loop(0, nb, block_mean, 0)

    row0 = pl.multiple_of(qi * blk, blk)
    bidx = lax.broadcasted_iota(jnp.int32, (nb, blk), 0)
    kpos = lax.broadcasted_iota(jnp.int32, (blk, blk), 0)
    qpos = lax.broadcasted_iota(jnp.int32, (blk, blk), 1)
    m_init, l_init, off_init, acc_init = [], [], [], []
    for h in range(heads):
        gate = _dot_nt(kbar_ref[:, cols[h]].astype(BF16), qs[h])
        gate = jnp.where(bidx < qi, gate, -jnp.inf)
        sel = jnp.zeros((nb, blk), dtype=jnp.bool_)
        for _ in range(MOBA_TOPK):
            best = jnp.max(gate, axis=0, keepdims=True)
            first = jnp.min(jnp.where(gate == best, bidx, nb), axis=0, keepdims=True)
            pick = (bidx == first) & (best > -jnp.inf)
            sel = sel | pick
            gate = jnp.where(pick, -jnp.inf, gate)
        off_init.append(jnp.where(sel, 0.0, jnp.inf))

        s = _dot_nt(k_ref[pl.ds(row0, blk), cols[h]], qs[h])
        s = jnp.where(kpos <= qpos, s, -jnp.inf)
        m0 = jnp.max(s, axis=0, keepdims=True)
        p = jnp.exp2(s - m0)
        m_init.append(m0)
        l_init.append(jnp.sum(p, axis=0, keepdims=True))
        acc_init.append(_dot_tn(v_ref[pl.ds(row0, blk), cols[h]], p.astype(BF16)))
    for h in range(heads):
        off_ref[h] = off_init[h]
        acc_ref[h] = acc_init[h]

    def issue_scores(j, dst_ref):
        r = pl.multiple_of(j * blk, blk)
        maxima = []
        for h in range(heads):
            s = _dot_nt(k_ref[pl.ds(r, blk), cols[h]], qs[h])
            dst_ref[h] = s
            maxima.append(jnp.max(s, axis=0, keepdims=True))
        return tuple(maxima)

    def half_step(j, ms, ls, cur_max, cur_ref, nxt_ref):
        jc = jnp.minimum(j, qi - 1)
        r0 = pl.multiple_of(jc * blk, blk)
        live = j < qi
        offs = [jnp.where(live, off_ref[h, pl.ds(jc, 1), :], jnp.inf) for h in range(heads)]
        nxt_max = issue_scores(jnp.minimum(j + 1, qi - 1), nxt_ref)
        new_ms, new_ls = [], []
        for h in range(heads):
            sj = cur_ref[h]
            m_new = jnp.maximum(ms[h], cur_max[h] - offs[h])
            alpha = jnp.exp2(ms[h] - m_new)
            pj = jnp.exp2(sj - (m_new + offs[h]))
            new_ms.append(m_new)
            new_ls.append(alpha * ls[h] + jnp.sum(pj, axis=0, keepdims=True))
            acc_ref[h] = alpha * acc_ref[h] + _dot_tn(v_ref[pl.ds(r0, blk), cols[h]], pj.astype(BF16))
        return tuple(new_ms), tuple(new_ls), nxt_max

    def body(jj, carry):
        ms, ls, max_a = carry
        ms, ls, max_b = half_step(2 * jj, ms, ls, max_a, sa_ref, sb_ref)
        return half_step(2 * jj + 1, ms, ls, max_b, sb_ref, sa_ref)

    max_0 = issue_scores(0, sa_ref)
    _, ls, _ = lax.fori_loop(0, (qi + 1) // 2, body, (tuple(m_init), tuple(l_init), max_0))
    outs = [(acc_ref[h] * (1.0 / ls[h])).T for h in range(heads)]
    gates = _silu(z_ref[...].astype(F32))
    o_ref[...] = (jnp.concatenate(outs, axis=1) * gates).astype(o_ref.dtype)


def _moba(kvqz, *, batch, seq, heads=4):
    blk = MOBA_BLOCK
    nb = seq // blk
    w = heads * ATT_HEAD_DIM
    ngroups = ATT_HEADS // heads
    return pl.pallas_call(
        functools.partial(_moba_kernel, heads=heads),
        grid=(batch, ngroups, nb),
        in_specs=[
            pl.BlockSpec((blk, w), lambda b, h, i: (b * nb + i, 2 * ngroups + h)),
            pl.BlockSpec((seq, w), lambda b, h, i: (b, h)),
            pl.BlockSpec((seq, w), lambda b, h, i: (b, ngroups + h)),
            pl.BlockSpec((blk, w), lambda b, h, i: (b * nb + i, 3 * ngroups + h)),
        ],
        out_specs=pl.BlockSpec((blk, w), lambda b, h, i: (b * nb + i, h)),
        out_shape=jax.ShapeDtypeStruct((batch * seq, ATT_WIDTH), BF16),
        scratch_shapes=[pltpu.VMEM((nb, w), F32),
                        pltpu.VMEM((heads, nb, blk), F32),
                        pltpu.VMEM((heads, ATT_HEAD_DIM, blk), F32),
                        pltpu.VMEM((heads, blk, blk), F32),
                        pltpu.VMEM((heads, blk, blk), F32)],
        compiler_params=_cparams("parallel", "parallel", "arbitrary"),
        name="moba",
    )(kvqz, kvqz, kvqz, kvqz)


def _final_kernel(o_ref, w_ref, r_ref, g_ref, out_ref):
    x = r_ref[...] + _dot(o_ref[...], w_ref[...])
    out_ref[...] = x * _rms_scale(x) * g_ref[...]


def _final(o, w, res, g, *, tm=512):
    t, k = o.shape
    n = w.shape[1]
    return pl.pallas_call(
        _final_kernel,
        grid=(t // tm,),
        in_specs=[
            pl.BlockSpec((tm, k), lambda i: (i, 0)),
            pl.BlockSpec((k, n), lambda i: (0, 0)),
            pl.BlockSpec((tm, n), lambda i: (i, 0)),
            pl.BlockSpec((1, n), lambda i: (0, 0)),
        ],
        out_specs=pl.BlockSpec((tm, n), lambda i: (i, 0)),
        out_shape=jax.ShapeDtypeStruct((t, n), F32),
        compiler_params=_cparams("parallel"),
        name="final",
    )(o, w, res, g)


def _rope_tables(seq):
    hd = ATT_HEAD_DIM
    inv_freq = ROPE_THETA ** (-jnp.arange(0, hd, 2, dtype=F32) / hd)
    ang = jnp.arange(seq, dtype=F32)[:, None] * inv_freq[None, :]
    cos, sin = jnp.cos(ang), jnp.sin(ang)
    return jnp.concatenate([cos, cos], axis=1), jnp.concatenate([-sin, sin], axis=1)


def kernel(x, a_norm_g, a_w_in, a_conv_w, a_conv_b, a_dt_bias, a_A_log, a_D, a_gnorm_g, a_w_out,
           kv_norm_g, w_kv, b_norm_g, b_w_in, b_w_out, final_norm_g):
    batch, seq, d = x.shape
    assert d == D_MODEL and seq % SSD_CHUNK == 0 and seq % MOBA_BLOCK == 0
    assert a_w_in.shape[0] == 1 and b_w_in.shape[0] == 1
    t = batch * seq
    G, R, N = SSM_GROUPS, HEADS_PER_GROUP, SSM_STATE
    x2d = x.reshape(t, d)

    w_in = a_w_in[0]
    n_main = 2 * D_INNER + 2 * G * N
    w_t = w_in.T.astype(BF16)
    w_dt_t = w_t[n_main:, :]

    def group_major(p):
        rows = p.shape[0]
        px = p[:, :D_INNER].reshape(rows, G, GROUP_WIDTH)
        pb = p[:, D_INNER:D_INNER + G * N].reshape(rows, G, N)
        pcc = p[:, D_INNER + G * N:].reshape(rows, G, N)
        return jnp.transpose(jnp.concatenate([px, pb, pcc], axis=2), (1, 0, 2))

    cw = group_major(a_conv_w[0])
    cb = group_major(a_conv_b[0][None, :])
    dt_bias = a_dt_bias[0].reshape(G, R)
    a_log = a_A_log[0].reshape(G, R)
    pr = jnp.stack([dt_bias, a_log], axis=2)
    dexp = jnp.repeat(a_D[0], SSM_HEAD_DIM).reshape(G, 1, GROUP_WIDTH)
    gn = a_gnorm_g[0].reshape(G, 1, GROUP_WIDTH)
    tril = jnp.tril(jnp.ones((SSD_CHUNK, SSD_CHUNK), BF16))
    expand = jnp.repeat(jnp.eye(R, dtype=BF16), SSM_HEAD_DIM, axis=1)
    shift = jnp.concatenate([jnp.eye(SSD_CHUNK, k=-dd, dtype=BF16) for dd in range(1, CONV_K)], axis=0)

    proj, dt_t = _a_in_proj(x2d, a_norm_g[0][None, :], w_t, w_dt_t, n=n_main)
    y = _ssd(proj, dt_t, cw, cb, pr, dexp, gn, tril, expand, shift, batch=batch, seq=seq)
    x1, xn = _proj_res(y, a_w_out[0].astype(BF16), x2d)

    cos, sin = _rope_tables(seq)
    w_kv_g = (kv_norm_g[:, None] * w_kv).astype(BF16)
    w_qz_g = (b_norm_g[0][:, None] * b_w_in[0]).astype(BF16)
    kvqz = _b_proj(xn, w_kv_g, w_qz_g, cos, sin, seq=seq)
    o = _moba(kvqz, batch=batch, seq=seq)
    out = _final(o, b_w_out[0].astype(BF16), x1, final_norm_g[None, :])
    return out.reshape(batch, seq, d)
```

```python
import functools

import jax
import jax.numpy as jnp
import numpy as np
from jax import lax
from jax.experimental import pallas as pl
from jax.experimental.pallas import tpu as pltpu

F32 = jnp.float32
BF16 = jnp.bfloat16

D_MODEL = 2048
SSM_HEAD_DIM = 64
SSM_GROUPS = 8
SSM_STATE = 128
CONV_K = 4
SSD_CHUNK = 256
ATT_HEADS = 16
ATT_HEAD_DIM = 128
MOBA_BLOCK = 256
MOBA_TOPK = 3
ROPE_THETA = 10000.0
EPS = 1e-5

D_INNER = 2 * D_MODEL
SSM_HEADS = D_INNER // SSM_HEAD_DIM
HEADS_PER_GROUP = SSM_HEADS // SSM_GROUPS
GROUP_WIDTH = HEADS_PER_GROUP * SSM_HEAD_DIM
XBC_WIDTH = GROUP_WIDTH + 2 * SSM_STATE
ATT_WIDTH = ATT_HEADS * ATT_HEAD_DIM
LOG2E = 1.4426950408889634
QK_SCALE_LOG2E = (ATT_HEAD_DIM ** -0.5) * LOG2E
HALO = 8

LANES = 128
VMEM_LIMIT = 60 * 1024 * 1024


def _cparams(*sem):
    return pltpu.CompilerParams(dimension_semantics=sem, vmem_limit_bytes=VMEM_LIMIT)


def _rms_scale(x):
    return lax.rsqrt(jnp.mean(x * x, axis=-1, keepdims=True) + EPS)


def _silu(x):
    return x * jax.nn.sigmoid(x)


def _softplus(x):
    return jnp.maximum(x, 0.0) + jnp.log1p(jnp.exp(-jnp.abs(x)))


def _dot(a, b):
    return jnp.dot(a, b, preferred_element_type=F32)


def _dot_nt(a, b):
    return lax.dot_general(a, b, (((1,), (1,)), ((), ())), preferred_element_type=F32)


def _dot_tn(a, b):
    return lax.dot_general(a, b, (((0,), (0,)), ((), ())), preferred_element_type=F32)


def _a_in_proj_kernel(x_ref, g_ref, w_ref, wdt_ref, o_ref, dtt_ref, hn_ref):
    @pl.when(pl.program_id(1) == 0)
    def _():
        x = x_ref[...]
        hn = (x * _rms_scale(x) * g_ref[...]).astype(BF16)
        hn_ref[...] = hn
        dtt_ref[...] = _dot_nt(wdt_ref[...], hn)

    o_ref[...] = _dot_nt(hn_ref[...], w_ref[...]).astype(o_ref.dtype)


def _a_in_proj(x2d, g, w_t, w_dt_t, *, n, tm=1024, tn=2048):
    t, d = x2d.shape
    nh = w_dt_t.shape[0]
    return pl.pallas_call(
        _a_in_proj_kernel,
        grid=(t // tm, n // tn),
        in_specs=[
            pl.BlockSpec((tm, d), lambda i, j: (i, 0)),
            pl.BlockSpec((1, d), lambda i, j: (0, 0)),
            pl.BlockSpec((tn, d), lambda i, j: (j, 0)),
            pl.BlockSpec((nh, d), lambda i, j: (0, 0)),
        ],
        out_specs=[
            pl.BlockSpec((tm, tn), lambda i, j: (i, j)),
            pl.BlockSpec((nh, tm), lambda i, j: (0, i)),
        ],
        out_shape=[
            jax.ShapeDtypeStruct((t, n), BF16),
            jax.ShapeDtypeStruct((nh, t), F32),
        ],
        scratch_shapes=[pltpu.VMEM((tm, d), BF16)],
        compiler_params=_cparams("parallel", "arbitrary"),
        name="a_in_proj",
    )(x2d, g, w_t, w_dt_t)


def _split3(a):
    a1 = a.astype(BF16)
    r1 = a - a1.astype(F32)
    a2 = r1.astype(BF16)
    a3 = (r1 - a2.astype(F32)).astype(BF16)
    return a1, a2, a3


def _ssd_group(z_b, x_b, b_b, c_b, dt_raw, cw, cb, pr, dexp, gn, tril, expand, shift, state, tail):
    L = SSD_CHUNK

    raw_b = jnp.concatenate([x_b, b_b, c_b], axis=1)
    raw = raw_b.astype(F32)
    shifted = _dot(shift, raw_b)
    edge = jnp.concatenate([tail, raw[:HALO, :]], axis=0)
    acc = cb + cw[CONV_K - 1:CONV_K, :] * raw
    for k in range(CONV_K - 1):
        d = CONV_K - 1 - k
        tap = jnp.concatenate([edge[HALO - d:2 * HALO - d, :], shifted[(d - 1) * L + HALO:d * L, :]], axis=0)
        acc = acc + cw[k:k + 1, :] * tap
    u = _silu(acc)
    xs = u[:, :GROUP_WIDTH]
    bm = u[:, GROUP_WIDTH:GROUP_WIDTH + SSM_STATE].astype(BF16)
    cm = u[:, GROUP_WIDTH + SSM_STATE:].astype(BF16)

    dtr = _softplus(dt_raw + pr[:, 0:1])
    a_pieces = _split3(dtr * (-jnp.exp(pr[:, 1:2])) * LOG2E)
    cum_r = sum(_dot_nt(piece, tril) for piece in a_pieces)
    cum_c = sum(_dot_nt(tril, piece) for piece in a_pieces)
    e_c = jnp.exp2(cum_c)
    dend_c = jnp.exp2(cum_c[L - 1:L, :] - cum_c)

    dt_x = _dot_tn(dtr.astype(BF16), expand)
    e_x = _dot(e_c.astype(BF16), expand)
    dend_x = _dot(dend_c.astype(BF16), expand)
    e_last_x = sum(_dot(piece, expand) for piece in _split3(e_c[L - HALO:, :]))[HALO - 1:HALO, :]
    lane = lax.broadcasted_iota(jnp.int32, (L, LANES), 1)

    xdt = xs * dt_x
    xdt_b = xdt.astype(BF16)

    cbm = _dot_nt(cm, bm)
    li = lax.broadcasted_iota(jnp.int32, (L, L), 0)
    si = lax.broadcasted_iota(jnp.int32, (L, L), 1)
    causal = si <= li
    y_parts = []
    for p in range(HEADS_PER_GROUP // 2):
        xp = xdt_b[:, p * LANES:(p + 1) * LANES]
        ys = []
        for r in (2 * p, 2 * p + 1):
            seg = cum_c[:, r:r + 1] - cum_r[r:r + 1, :]
            m = (cbm * jnp.exp2(jnp.where(causal, seg, -jnp.inf))).astype(BF16)
            ys.append(_dot(m, xp))
        y_parts.append(jnp.where(lane < SSM_HEAD_DIM, ys[0], ys[1]))
    y = jnp.concatenate(y_parts, axis=1)

    y = y + _dot(cm, state.astype(BF16)) * e_x
    new_state = state * e_last_x + _dot_tn(bm, (xdt * dend_x).astype(BF16))

    y = y + dexp * xs
    y = y * _silu(z_b.astype(F32))
    return y * _rms_scale(y) * gn, new_state, raw[L - HALO:, :]


def _ssd_kernel(z_ref, x_ref, b_ref, c_ref, dtr_ref, cw_ref, cb_ref, pr_ref, dexp_ref,
                gn_ref, tril_ref, expand_ref, shift_ref, o_ref, state_ref, tail_ref, *, groups):
    @pl.when(pl.program_id(2) == 0)
    def _():
        state_ref[...] = jnp.zeros_like(state_ref)
        tail_ref[...] = jnp.zeros_like(tail_ref)

    tril = tril_ref[...]
    expand = expand_ref[...]
    shift = shift_ref[...]
    results = []
    for i in range(groups):
        wide = slice(i * GROUP_WIDTH, (i + 1) * GROUP_WIDTH)
        narrow = slice(i * SSM_STATE, (i + 1) * SSM_STATE)
        heads = slice(i * HEADS_PER_GROUP, (i + 1) * HEADS_PER_GROUP)
        results.append(_ssd_group(
            z_ref[:, wide], x_ref[:, wide], b_ref[:, narrow], c_ref[:, narrow], dtr_ref[heads, :],
            cw_ref[i], cb_ref[i], pr_ref[i], dexp_ref[i], gn_ref[i], tril, expand, shift,
            state_ref[i], tail_ref[i]))
    for i, (y, new_state, new_tail) in enumerate(results):
        o_ref[:, i * GROUP_WIDTH:(i + 1) * GROUP_WIDTH] = y.astype(o_ref.dtype)
        state_ref[i] = new_state
        tail_ref[i] = new_tail


def _ssd(proj, dt_t, cw, cb, pr, dexp, gn, tril, expand, shift, *, batch, seq, groups=8):
    L = SSD_CHUNK
    nc = seq // L
    x_block0 = D_INNER // (groups * GROUP_WIDTH)
    b_block0 = 2 * D_INNER // (groups * SSM_STATE)
    c_block0 = b_block0 + SSM_GROUPS // groups
    return pl.pallas_call(
        functools.partial(_ssd_kernel, groups=groups),
        grid=(batch, SSM_GROUPS // groups, nc),
        in_specs=[
            pl.BlockSpec((L, groups * GROUP_WIDTH), lambda b, g, c: (b * nc + c, g)),
            pl.BlockSpec((L, groups * GROUP_WIDTH), lambda b, g, c: (b * nc + c, x_block0 + g)),
            pl.BlockSpec((L, groups * SSM_STATE), lambda b, g, c: (b * nc + c, b_block0 + g)),
            pl.BlockSpec((L, groups * SSM_STATE), lambda b, g, c: (b * nc + c, c_block0 + g)),
            pl.BlockSpec((groups * HEADS_PER_GROUP, L), lambda b, g, c: (g, b * nc + c)),
            pl.BlockSpec((groups, CONV_K, XBC_WIDTH), lambda b, g, c: (g, 0, 0)),
            pl.BlockSpec((groups, 1, XBC_WIDTH), lambda b, g, c: (g, 0, 0)),
            pl.BlockSpec((groups, HEADS_PER_GROUP, 2), lambda b, g, c: (g, 0, 0)),
            pl.BlockSpec((groups, 1, GROUP_WIDTH), lambda b, g, c: (g, 0, 0)),
            pl.BlockSpec((groups, 1, GROUP_WIDTH), lambda b, g, c: (g, 0, 0)),
            pl.BlockSpec((L, L), lambda b, g, c: (0, 0)),
            pl.BlockSpec((HEADS_PER_GROUP, GROUP_WIDTH), lambda b, g, c: (0, 0)),
            pl.BlockSpec(((CONV_K - 1) * L, L), lambda b, g, c: (0, 0)),
        ],
        out_specs=pl.BlockSpec((L, groups * GROUP_WIDTH), lambda b, g, c: (b * nc + c, g)),
        out_shape=jax.ShapeDtypeStruct((batch * seq, D_INNER), BF16),
        scratch_shapes=[pltpu.VMEM((groups, SSM_STATE, GROUP_WIDTH), F32),
                        pltpu.VMEM((groups, HALO, XBC_WIDTH), F32)],
        compiler_params=_cparams("parallel", "parallel", "arbitrary"),
        name="ssd",
    )(proj, proj, proj, proj, dt_t, cw, cb, pr, dexp, gn, tril, expand, shift)


def _proj_res_kernel(y_ref, w_ref, r_ref, x1_ref, xn_ref):
    x1 = r_ref[...] + _dot(y_ref[...], w_ref[...])
    x1_ref[...] = x1
    xn_ref[...] = (x1 * _rms_scale(x1)).astype(xn_ref.dtype)


def _proj_res(y, w, res, *, tm=512):
    t, k = y.shape
    n = w.shape[1]
    return pl.pallas_call(
        _proj_res_kernel,
        grid=(t // tm,),
        in_specs=[
            pl.BlockSpec((tm, k), lambda i: (i, 0)),
            pl.BlockSpec((k, n), lambda i: (0, 0), pipeline_mode=pl.Buffered(1)),
            pl.BlockSpec((tm, n), lambda i: (i, 0)),
        ],
        out_specs=[
            pl.BlockSpec((tm, n), lambda i: (i, 0)),
            pl.BlockSpec((tm, n), lambda i: (i, 0)),
        ],
        out_shape=[
            jax.ShapeDtypeStruct((t, n), F32),
            jax.ShapeDtypeStruct((t, n), BF16),
        ],
        compiler_params=_cparams("parallel"),
        name="a_out_proj",
    )(y, w, res)


def _b_proj_kernel(xn_ref, wkv_ref, wqz_ref, cos_ref, sin_ref, o_ref, *, tiles_per_section):
    tm, tn = o_ref.shape
    section = pl.program_id(0) // tiles_per_section

    is_rope = section % 2 == 0
    scale = jnp.where(section == 2, QK_SCALE_LOG2E, 1.0)
    cos = jnp.where(is_rope, cos_ref[...] * scale, 1.0)
    sin = jnp.where(is_rope, sin_ref[...] * scale, 0.0)

    def project(w_ref):
        acc = _dot(xn_ref[...], w_ref[...])
        for h in range(tn // ATT_HEAD_DIM):
            sl = slice(h * ATT_HEAD_DIM, (h + 1) * ATT_HEAD_DIM)
            a = acc[:, sl]
            o_ref[:, sl] = (a * cos + pltpu.roll(a, ATT_HEAD_DIM // 2, axis=1) * sin).astype(o_ref.dtype)

    pl.when(section < 2)(functools.partial(project, wkv_ref))
    pl.when(section >= 2)(functools.partial(project, wqz_ref))


def _b_proj(xn, w_kv, w_qz, cos, sin, *, seq, tm=1024, tn=2048):
    t, d = xn.shape
    kv_tiles = w_kv.shape[1] // tn
    n_tiles = kv_tiles + w_qz.shape[1] // tn
    tiles_per_section = ATT_WIDTH // tn
    s_tiles = seq // tm
    return pl.pallas_call(
        functools.partial(_b_proj_kernel, tiles_per_section=tiles_per_section),
        grid=(n_tiles, t // tm),
        in_specs=[
            pl.BlockSpec((tm, d), lambda j, i: (i, 0)),
            pl.BlockSpec((d, tn), lambda j, i: (0, jnp.minimum(j, kv_tiles - 1))),
            pl.BlockSpec((d, tn), lambda j, i: (0, jnp.maximum(j - kv_tiles, 0))),
            pl.BlockSpec((tm, ATT_HEAD_DIM), lambda j, i: (i % s_tiles, 0)),
            pl.BlockSpec((tm, ATT_HEAD_DIM), lambda j, i: (i % s_tiles, 0)),
        ],
        out_specs=pl.BlockSpec((tm, tn), lambda j, i: (i, j)),
        out_shape=jax.ShapeDtypeStruct((t, n_tiles * tn), BF16),
        compiler_params=_cparams("parallel", "parallel"),
        name="b_proj",
    )(xn, w_kv, w_qz, cos, sin)


def _moba_kernel(q_ref, k_ref, v_ref, z_ref, o_ref, kbar_ref, off_ref, acc_ref, sa_ref, sb_ref, *, heads):
    qi = pl.program_id(2)
    blk = MOBA_BLOCK
    hd = ATT_HEAD_DIM
    nb = kbar_ref.shape[0]
    cols = [slice(h * hd, (h + 1) * hd) for h in range(heads)]
    qs = [q_ref[:, cs] for cs in cols]

    @pl.when(qi == 0)
    def _():
        def block_mean(n, carry):
            r = pl.multiple_of(n * blk, blk)
            kbar_ref[pl.ds(n, 1), :] = jnp.mean(k_ref[pl.ds(r, blk), :].astype(F32), axis=0, keepdims=True)
            return carry
        lax.fori_loop(0, nb, block_mean, 0)

    row0 = pl.multiple_of(qi * blk, blk)
    bidx = lax.broadcasted_iota(jnp.int32, (nb, blk), 0)
    kpos = lax.broadcasted_iota(jnp.int32, (blk, blk), 0)
    qpos = lax.broadcasted_iota(jnp.int32, (blk, blk), 1)
    m_init, l_init, off_init, acc_init = [], [], [], []
    for h in range(heads):
        gate = _dot_nt(kbar_ref[:, cols[h]].astype(BF16), qs[h])
        gate = jnp.where(bidx < qi, gate, -jnp.inf)
        sel = jnp.zeros((nb, blk), dtype=jnp.bool_)
        for _ in range(MOBA_TOPK):
            best = jnp.max(gate, axis=0, keepdims=True)
            first = jnp.min(jnp.where(gate == best, bidx, nb), axis=0, keepdims=True)
            pick = (bidx == first) & (best > -jnp.inf)
            sel = sel | pick
            gate = jnp.where(pick, -jnp.inf, gate)
        off_init.append(jnp.where(sel, 0.0, jnp.inf))

        s = _dot_nt(k_ref[pl.ds(row0, blk), cols[h]], qs[h])
        s = jnp.where(kpos <= qpos, s, -jnp.inf)
        m0 = jnp.max(s, axis=0, keepdims=True)
        p = jnp.exp2(s - m0)
        m_init.append(m0)
        l_init.append(jnp.sum(p, axis=0, keepdims=True))
        acc_init.append(_dot_tn(v_ref[pl.ds(row0, blk), cols[h]], p.astype(BF16)))
    for h in range(heads):
        off_ref[h] = off_init[h]
        acc_ref[h] = acc_init[h]

    def issue_scores(j, dst_ref):
        r = pl.multiple_of(j * blk, blk)
        maxima = []
        for h in range(heads):
            s = _dot_nt(k_ref[pl.ds(r, blk), cols[h]], qs[h])
            dst_ref[h] = s
            maxima.append(jnp.max(s, axis=0, keepdims=True))
        return tuple(maxima)

    def half_step(j, ms, ls, cur_max, cur_ref, nxt_ref):
        jc = jnp.minimum(j, qi - 1)
        r0 = pl.multiple_of(jc * blk, blk)
        live = j < qi
        offs = [jnp.where(live, off_ref[h, pl.ds(jc, 1), :], jnp.inf) for h in range(heads)]
        nxt_max = issue_scores(jnp.minimum(j + 1, qi - 1), nxt_ref)
        new_ms, new_ls = [], []
        for h in range(heads):
            sj = cur_ref[h]
            m_new = jnp.maximum(ms[h], cur_max[h] - offs[h])
            alpha = jnp.exp2(ms[h] - m_new)
            pj = jnp.exp2(sj - (m_new + offs[h]))
            new_ms.append(m_new)
            new_ls.append(alpha * ls[h] + jnp.sum(pj, axis=0, keepdims=True))
            acc_ref[h] = alpha * acc_ref[h] + _dot_tn(v_ref[pl.ds(r0, blk), cols[h]], pj.astype(BF16))
        return tuple(new_ms), tuple(new_ls), nxt_max

    def body(jj, carry):
        ms, ls, max_a = carry
        ms, ls, max_b = half_step(2 * jj, ms, ls, max_a, sa_ref, sb_ref)
        return half_step(2 * jj + 1, ms, ls, max_b, sb_ref, sa_ref)

    max_0 = issue_scores(0, sa_ref)
    _, ls, _ = lax.fori_loop(0, (qi + 1) // 2, body, (tuple(m_init), tuple(l_init), max_0))
    outs = [(acc_ref[h] * (1.0 / ls[h])).T for h in range(heads)]
    gates = _silu(z_ref[...].astype(F32))
    o_ref[...] = (jnp.concatenate(outs, axis=1) * gates).astype(o_ref.dtype)


def _moba(kvqz, *, batch, seq, heads=4):
    blk = MOBA_BLOCK
    nb = seq // blk
    w = heads * ATT_HEAD_DIM
    ngroups = ATT_HEADS // heads
    return pl.pallas_call(
        functools.partial(_moba_kernel, heads=heads),
        grid=(batch, ngroups, nb),
        in_specs=[
            pl.BlockSpec((blk, w), lambda b, h, i: (b * nb + i, 2 * ngroups + h)),
            pl.BlockSpec((seq, w), lambda b, h, i: (b, h)),
            pl.BlockSpec((seq, w), lambda b, h, i: (b, ngroups + h)),
            pl.BlockSpec((blk, w), lambda b, h, i: (b * nb + i, 3 * ngroups + h)),
        ],
        out_specs=pl.BlockSpec((blk, w), lambda b, h, i: (b * nb + i, h)),
        out_shape=jax.ShapeDtypeStruct((batch * seq, ATT_WIDTH), BF16),
        scratch_shapes=[pltpu.VMEM((nb, w), F32),
                        pltpu.VMEM((heads, nb, blk), F32),
                        pltpu.VMEM((heads, ATT_HEAD_DIM, blk), F32),
                        pltpu.VMEM((heads, blk, blk), F32),
                        pltpu.VMEM((heads, blk, blk), F32)],
        compiler_params=_cparams("parallel", "parallel", "arbitrary"),
        name="moba",
    )(kvqz, kvqz, kvqz, kvqz)


def _final_kernel(o_ref, w_ref, r_ref, g_ref, out_ref):
    x = r_ref[...] + _dot(o_ref[...], w_ref[...])
    out_ref[...] = x * _rms_scale(x) * g_ref[...]


def _final(o, w, res, g, *, tm=512):
    t, k = o.shape
    n = w.shape[1]
    return pl.pallas_call(
        _final_kernel,
        grid=(t // tm,),
        in_specs=[
            pl.BlockSpec((tm, k), lambda i: (i, 0)),
            pl.BlockSpec((k, n), lambda i: (0, 0)),
            pl.BlockSpec((tm, n), lambda i: (i, 0)),
            pl.BlockSpec((1, n), lambda i: (0, 0)),
        ],
        out_specs=pl.BlockSpec((tm, n), lambda i: (i, 0)),
        out_shape=jax.ShapeDtypeStruct((t, n), F32),
        compiler_params=_cparams("parallel"),
        name="final",
    )(o, w, res, g)


def _rope_tables(seq):
    hd = ATT_HEAD_DIM
    inv_freq = ROPE_THETA ** (-np.arange(0, hd, 2, dtype=np.float64) / hd)
    ang = np.arange(seq, dtype=np.float64)[:, None] * inv_freq[None, :]
    cos, sin = np.cos(ang), np.sin(ang)
    return (jnp.asarray(np.concatenate([cos, cos], axis=1), F32),
            jnp.asarray(np.concatenate([-sin, sin], axis=1), F32))


def kernel(x, a_norm_g, a_w_in, a_conv_w, a_conv_b, a_dt_bias, a_A_log, a_D, a_gnorm_g, a_w_out,
           kv_norm_g, w_kv, b_norm_g, b_w_in, b_w_out, final_norm_g):
    batch, seq, d = x.shape
    assert d == D_MODEL and seq % SSD_CHUNK == 0 and seq % MOBA_BLOCK == 0
    assert a_w_in.shape[0] == 1 and b_w_in.shape[0] == 1
    t = batch * seq
    G, R, N = SSM_GROUPS, HEADS_PER_GROUP, SSM_STATE
    x2d = x.reshape(t, d)

    w_in = a_w_in[0]
    n_main = 2 * D_INNER + 2 * G * N
    w_t = w_in.T.astype(BF16)
    w_dt_t = w_t[n_main:, :]

    def group_major(p):
        rows = p.shape[0]
        px = p[:, :D_INNER].reshape(rows, G, GROUP_WIDTH)
        pb = p[:, D_INNER:D_INNER + G * N].reshape(rows, G, N)
        pcc = p[:, D_INNER + G * N:].reshape(rows, G, N)
        return jnp.transpose(jnp.concatenate([px, pb, pcc], axis=2), (1, 0, 2))

    cw = group_major(a_conv_w[0])
    cb = group_major(a_conv_b[0][None, :])
    dt_bias = a_dt_bias[0].reshape(G, R)
    a_log = a_A_log[0].reshape(G, R)
    pr = jnp.stack([dt_bias, a_log], axis=2)
    dexp = jnp.repeat(a_D[0], SSM_HEAD_DIM).reshape(G, 1, GROUP_WIDTH)
    gn = a_gnorm_g[0].reshape(G, 1, GROUP_WIDTH)
    tril = jnp.tril(jnp.ones((SSD_CHUNK, SSD_CHUNK), BF16))
    expand = jnp.repeat(jnp.eye(R, dtype=BF16), SSM_HEAD_DIM, axis=1)
    shift = jnp.concatenate([jnp.eye(SSD_CHUNK, k=-dd, dtype=BF16) for dd in range(1, CONV_K)], axis=0)

    proj, dt_t = _a_in_proj(x2d, a_norm_g[0][None, :], w_t, w_dt_t, n=n_main)
    y = _ssd(proj, dt_t, cw, cb, pr, dexp, gn, tril, expand, shift, batch=batch, seq=seq)
    x1, xn = _proj_res(y, a_w_out[0].astype(BF16), x2d)

    cos, sin = _rope_tables(seq)
    w_kv_g = (kv_norm_g[:, None] * w_kv).astype(BF16)
    w_qz_g = (b_norm_g[0][:, None] * b_w_in[0]).astype(BF16)
    kvqz = _b_proj(xn, w_kv_g, w_qz_g, cos, sin, seq=seq)
    o = _moba(kvqz, batch=batch, seq=seq)
    out = _final(o, b_w_out[0].astype(BF16), x1, final_norm_g[None, :])
    return out.reshape(batch, seq, d)
```

```python
import functools

import jax
import jax.numpy as jnp
import numpy as np
from jax import lax
from jax.experimental import pallas as pl
from jax.experimental.pallas import tpu as pltpu

F32 = jnp.float32
BF16 = jnp.bfloat16

D_MODEL = 2048
SSM_HEAD_DIM = 64
SSM_GROUPS = 8
SSM_STATE = 128
CONV_K = 4
SSD_CHUNK = 256
ATT_HEADS = 16
ATT_HEAD_DIM = 128
MOBA_BLOCK = 256
MOBA_TOPK = 3
ROPE_THETA = 10000.0
EPS = 1e-5

D_INNER = 2 * D_MODEL
SSM_HEADS = D_INNER // SSM_HEAD_DIM
HEADS_PER_GROUP = SSM_HEADS // SSM_GROUPS
GROUP_WIDTH = HEADS_PER_GROUP * SSM_HEAD_DIM
XBC_WIDTH = GROUP_WIDTH + 2 * SSM_STATE
ATT_WIDTH = ATT_HEADS * ATT_HEAD_DIM
LOG2E = 1.4426950408889634
QK_SCALE_LOG2E = (ATT_HEAD_DIM ** -0.5) * LOG2E
HALO = 8

LANES = 128
VMEM_LIMIT = 60 * 1024 * 1024


def _cparams(*sem):
    return pltpu.CompilerParams(dimension_semantics=sem, vmem_limit_bytes=VMEM_LIMIT)


def _rms_scale(x):
    return lax.rsqrt(jnp.mean(x * x, axis=-1, keepdims=True) + EPS)


def _silu(x):
    return x * jax.nn.sigmoid(x)


def _softplus(x):
    return jnp.maximum(x, 0.0) + jnp.log1p(jnp.exp(-jnp.abs(x)))


def _dot(a, b):
    return jnp.dot(a, b, preferred_element_type=F32)


def _dot_nt(a, b):
    return lax.dot_general(a, b, (((1,), (1,)), ((), ())), preferred_element_type=F32)


def _dot_tn(a, b):
    return lax.dot_general(a, b, (((0,), (0,)), ((), ())), preferred_element_type=F32)


def _a_in_proj_kernel(x_ref, g_ref, w_ref, wdt_ref, o_ref, dtt_ref, hn_ref):
    @pl.when(pl.program_id(1) == 0)
    def _():
        x = x_ref[...]
        hn = (x * _rms_scale(x) * g_ref[...]).astype(BF16)
        hn_ref[...] = hn
        dtt_ref[...] = _dot_nt(wdt_ref[...], hn)

    o_ref[...] = _dot_nt(hn_ref[...], w_ref[...]).astype(o_ref.dtype)


def _a_in_proj(x2d, g, w_t, w_dt_t, *, n, tm=1024, tn=2048):
    t, d = x2d.shape
    nh = w_dt_t.shape[0]
    return pl.pallas_call(
        _a_in_proj_kernel,
        grid=(t // tm, n // tn),
        in_specs=[
            pl.BlockSpec((tm, d), lambda i, j: (i, 0)),
            pl.BlockSpec((1, d), lambda i, j: (0, 0)),
            pl.BlockSpec((tn, d), lambda i, j: (j, 0)),
            pl.BlockSpec((nh, d), lambda i, j: (0, 0)),
        ],
        out_specs=[
            pl.BlockSpec((tm, tn), lambda i, j: (i, j)),
            pl.BlockSpec((nh, tm), lambda i, j: (0, i)),
        ],
        out_shape=[
            jax.ShapeDtypeStruct((t, n), BF16),
            jax.ShapeDtypeStruct((nh, t), F32),
        ],
        scratch_shapes=[pltpu.VMEM((tm, d), BF16)],
        compiler_params=_cparams("parallel", "arbitrary"),
        name="a_in_proj",
    )(x2d, g, w_t, w_dt_t)


def _split3(a):
    a1 = a.astype(BF16)
    r1 = a - a1.astype(F32)
    a2 = r1.astype(BF16)
    a3 = (r1 - a2.astype(F32)).astype(BF16)
    return a1, a2, a3


def _ssd_group(z_b, x_b, b_b, c_b, dt_raw, cw, cb, pr, dexp, gn, tril, expand, shift, state, tail):
    L = SSD_CHUNK

    raw_b = jnp.concatenate([x_b, b_b, c_b], axis=1)
    raw = raw_b.astype(F32)
    shifted = _dot(shift, raw_b)
    edge = jnp.concatenate([tail, raw[:HALO, :]], axis=0)
    acc = cb + cw[CONV_K - 1:CONV_K, :] * raw
    for k in range(CONV_K - 1):
        d = CONV_K - 1 - k
        tap = jnp.concatenate([edge[HALO - d:2 * HALO - d, :], shifted[(d - 1) * L + HALO:d * L, :]], axis=0)
        acc = acc + cw[k:k + 1, :] * tap
    u = _silu(acc)
    xs = u[:, :GROUP_WIDTH]
    bm = u[:, GROUP_WIDTH:GROUP_WIDTH + SSM_STATE].astype(BF16)
    cm = u[:, GROUP_WIDTH + SSM_STATE:].astype(BF16)

    dtr = _softplus(dt_raw + pr[:, 0:1])
    a_pieces = _split3(dtr * (-jnp.exp(pr[:, 1:2])) * LOG2E)
    cum_r = sum(_dot_nt(piece, tril) for piece in a_pieces)
    cum_c = sum(_dot_nt(tril, piece) for piece in a_pieces)
    e_c = jnp.exp2(cum_c)
    dend_c = jnp.exp2(cum_c[L - 1:L, :] - cum_c)

    dt_x = _dot_tn(dtr.astype(BF16), expand)
    e_x = _dot(e_c.astype(BF16), expand)
    dend_x = _dot(dend_c.astype(BF16), expand)
    e_last_x = sum(_dot(piece, expand) for piece in _split3(e_c[L - HALO:, :]))[HALO - 1:HALO, :]
    lane = lax.broadcasted_iota(jnp.int32, (L, LANES), 1)

    xdt = xs * dt_x
    xdt_b = xdt.astype(BF16)

    cbm = _dot_nt(cm, bm)
    li = lax.broadcasted_iota(jnp.int32, (L, L), 0)
    si = lax.broadcasted_iota(jnp.int32, (L, L), 1)
    causal = si <= li
    y_parts = []
    for p in range(HEADS_PER_GROUP // 2):
        xp = xdt_b[:, p * LANES:(p + 1) * LANES]
        ys = []
        for r in (2 * p, 2 * p + 1):
            seg = cum_c[:, r:r + 1] - cum_r[r:r + 1, :]
            m = (cbm * jnp.exp2(jnp.where(causal, seg, -jnp.inf))).astype(BF16)
            ys.append(_dot(m, xp))
        y_parts.append(jnp.where(lane < SSM_HEAD_DIM, ys[0], ys[1]))
    y = jnp.concatenate(y_parts, axis=1)

    y = y + _dot(cm, state.astype(BF16)) * e_x
    new_state = state * e_last_x + _dot_tn(bm, (xdt * dend_x).astype(BF16))

    y = y + dexp * xs
    y = y * _silu(z_b.astype(F32))
    return y * _rms_scale(y) * gn, new_state, raw[L - HALO:, :]


def _ssd_kernel(z_ref, x_ref, b_ref, c_ref, dtr_ref, cw_ref, cb_ref, pr_ref, dexp_ref,
                gn_ref, tril_ref, expand_ref, shift_ref, o_ref, state_ref, tail_ref, *, groups):
    @pl.when(pl.program_id(2) == 0)
    def _():
        state_ref[...] = jnp.zeros_like(state_ref)
        tail_ref[...] = jnp.zeros_like(tail_ref)

    tril = tril_ref[...]
    expand = expand_ref[...]
    shift = shift_ref[...]
    results = []
    for i in range(groups):
        wide = slice(i * GROUP_WIDTH, (i + 1) * GROUP_WIDTH)
        narrow = slice(i * SSM_STATE, (i + 1) * SSM_STATE)
        heads = slice(i * HEADS_PER_GROUP, (i + 1) * HEADS_PER_GROUP)
        results.append(_ssd_group(
            z_ref[:, wide], x_ref[:, wide], b_ref[:, narrow], c_ref[:, narrow], dtr_ref[heads, :],
            cw_ref[i], cb_ref[i], pr_ref[i], dexp_ref[i], gn_ref[i], tril, expand, shift,
            state_ref[i], tail_ref[i]))
    for i, (y, new_state, new_tail) in enumerate(results):
        o_ref[:, i * GROUP_WIDTH:(i + 1) * GROUP_WIDTH] = y.astype(o_ref.dtype)
        state_ref[i] = new_state
        tail_ref[i] = new_tail


def _ssd(proj, dt_t, cw, cb, pr, dexp, gn, tril, expand, shift, *, batch, seq, groups=8):
    L = SSD_CHUNK
    nc = seq // L
    x_block0 = D_INNER // (groups * GROUP_WIDTH)
    b_block0 = 2 * D_INNER // (groups * SSM_STATE)
    c_block0 = b_block0 + SSM_GROUPS // groups
    return pl.pallas_call(
        functools.partial(_ssd_kernel, groups=groups),
        grid=(batch, SSM_GROUPS // groups, nc),
        in_specs=[
            pl.BlockSpec((L, groups * GROUP_WIDTH), lambda b, g, c: (b * nc + c, g)),
            pl.BlockSpec((L, groups * GROUP_WIDTH), lambda b, g, c: (b * nc + c, x_block0 + g)),
            pl.BlockSpec((L, groups * SSM_STATE), lambda b, g, c: (b * nc + c, b_block0 + g)),
            pl.BlockSpec((L, groups * SSM_STATE), lambda b, g, c: (b * nc + c, c_block0 + g)),
            pl.BlockSpec((groups * HEADS_PER_GROUP, L), lambda b, g, c: (g, b * nc + c)),
            pl.BlockSpec((groups, CONV_K, XBC_WIDTH), lambda b, g, c: (g, 0, 0)),
            pl.BlockSpec((groups, 1, XBC_WIDTH), lambda b, g, c: (g, 0, 0)),
            pl.BlockSpec((groups, HEADS_PER_GROUP, 2), lambda b, g, c: (g, 0, 0)),
            pl.BlockSpec((groups, 1, GROUP_WIDTH), lambda b, g, c: (g, 0, 0)),
            pl.BlockSpec((groups, 1, GROUP_WIDTH), lambda b, g, c: (g, 0, 0)),
            pl.BlockSpec((L, L), lambda b, g, c: (0, 0)),
            pl.BlockSpec((HEADS_PER_GROUP, GROUP_WIDTH), lambda b, g, c: (0, 0)),
            pl.BlockSpec(((CONV_K - 1) * L, L), lambda b, g, c: (0, 0)),
        ],
        out_specs=pl.BlockSpec((L, groups * GROUP_WIDTH), lambda b, g, c: (b * nc + c, g)),
        out_shape=jax.ShapeDtypeStruct((batch * seq, D_INNER), BF16),
        scratch_shapes=[pltpu.VMEM((groups, SSM_STATE, GROUP_WIDTH), F32),
                        pltpu.VMEM((groups, HALO, XBC_WIDTH), F32)],
        compiler_params=_cparams("parallel", "parallel", "arbitrary"),
        name="ssd",
    )(proj, proj, proj, proj, dt_t, cw, cb, pr, dexp, gn, tril, expand, shift)


def _proj_res_kernel(y_ref, w_ref, r_ref, x1_ref, xn_ref):
    x1 = r_ref[...] + _dot(y_ref[...], w_ref[...])
    x1_ref[...] = x1
    xn_ref[...] = (x1 * _rms_scale(x1)).astype(xn_ref.dtype)


def _proj_res(y, w, res, *, tm=512):
    t, k = y.shape
    n = w.shape[1]
    return pl.pallas_call(
        _proj_res_kernel,
        grid=(t // tm,),
        in_specs=[
            pl.BlockSpec((tm, k), lambda i: (i, 0)),
            pl.BlockSpec((k, n), lambda i: (0, 0), pipeline_mode=pl.Buffered(1)),
            pl.BlockSpec((tm, n), lambda i: (i, 0)),
        ],
        out_specs=[
            pl.BlockSpec((tm, n), lambda i: (i, 0)),
            pl.BlockSpec((tm, n), lambda i: (i, 0)),
        ],
        out_shape=[
            jax.ShapeDtypeStruct((t, n), F32),
            jax.ShapeDtypeStruct((t, n), BF16),
        ],
        compiler_params=_cparams("parallel"),
        name="a_out_proj",
    )(y, w, res)


def _b_proj_kernel(xn_ref, wkv_ref, wqz_ref, cos_ref, sin_ref, o_ref, *, tiles_per_section):
    tm, tn = o_ref.shape
    section = pl.program_id(0) // tiles_per_section

    is_rope = section % 2 == 0
    scale = jnp.where(section == 2, QK_SCALE_LOG2E, 1.0)
    cos = jnp.where(is_rope, cos_ref[...] * scale, 1.0)
    sin = jnp.where(is_rope, sin_ref[...] * scale, 0.0)

    def project(w_ref):
        acc = _dot(xn_ref[...], w_ref[...])
        for h in range(tn // ATT_HEAD_DIM):
            sl = slice(h * ATT_HEAD_DIM, (h + 1) * ATT_HEAD_DIM)
            a = acc[:, sl]
            o_ref[:, sl] = (a * cos + pltpu.roll(a, ATT_HEAD_DIM // 2, axis=1) * sin).astype(o_ref.dtype)

    pl.when(section < 2)(functools.partial(project, wkv_ref))
    pl.when(section >= 2)(functools.partial(project, wqz_ref))


def _b_proj(xn, w_kv, w_qz, cos, sin, *, seq, tm=1024, tn=2048):
    t, d = xn.shape
    kv_tiles = w_kv.shape[1] // tn
    n_tiles = kv_tiles + w_qz.shape[1] // tn
    tiles_per_section = ATT_WIDTH // tn
    s_tiles = seq // tm
    return pl.pallas_call(
        functools.partial(_b_proj_kernel, tiles_per_section=tiles_per_section),
        grid=(n_tiles, t // tm),
        in_specs=[
            pl.BlockSpec((tm, d), lambda j, i: (i, 0)),
            pl.BlockSpec((d, tn), lambda j, i: (0, jnp.minimum(j, kv_tiles - 1))),
            pl.BlockSpec((d, tn), lambda j, i: (0, jnp.maximum(j - kv_tiles, 0))),
            pl.BlockSpec((tm, ATT_HEAD_DIM), lambda j, i: (i % s_tiles, 0)),
            pl.BlockSpec((tm, ATT_HEAD_DIM), lambda j, i: (i % s_tiles, 0)),
        ],
        out_specs=pl.BlockSpec((tm, tn), lambda j, i: (i, j)),
        out_shape=jax.ShapeDtypeStruct((t, n_tiles * tn), BF16),
        compiler_params=_cparams("parallel", "parallel"),
        name="b_proj",
    )(xn, w_kv, w_qz, cos, sin)


def _moba_kernel(q_ref, k_ref, v_ref, z_ref, o_ref, kbar_ref, off_ref, acc_ref, sa_ref, sb_ref, *, heads):
    qi = pl.program_id(2)
    blk = MOBA_BLOCK
    hd = ATT_HEAD_DIM
    nb = kbar_ref.shape[0]
    cols = [slice(h * hd, (h + 1) * hd) for h in range(heads)]
    qs = [q_ref[:, cs] for cs in cols]

    @pl.when(qi == 0)
    def _():
        def block_mean(n, carry):
            r = pl.multiple_of(n * blk, blk)
            kbar_ref[pl.ds(n, 1), :] = jnp.mean(k_ref[pl.ds(r, blk), :].astype(F32), axis=0, keepdims=True)
            return carry
        lax.fori_loop(0, nb, block_mean, 0)

    row0 = pl.multiple_of(qi * blk, blk)
    bidx = lax.broadcasted_iota(jnp.int32, (nb, blk), 0)
    kpos = lax.broadcasted_iota(jnp.int32, (blk, blk), 0)
    qpos = lax.broadcasted_iota(jnp.int32, (blk, blk), 1)
    m_init, l_init, off_init, acc_init = [], [], [], []
    for h in range(heads):
        gate = _dot_nt(kbar_ref[:, cols[h]].astype(BF16), qs[h])
        gate = jnp.where(bidx < qi, gate, -jnp.inf)
        sel = jnp.zeros((nb, blk), dtype=jnp.bool_)
        for _ in range(MOBA_TOPK):
            best = jnp.max(gate, axis=0, keepdims=True)
            first = jnp.min(jnp.where(gate == best, bidx, nb), axis=0, keepdims=True)
            pick = (bidx == first) & (best > -jnp.inf)
            sel = sel | pick
            gate = jnp.where(pick, -jnp.inf, gate)
        off_init.append(jnp.where(sel, 0.0, jnp.inf))

        s = _dot_nt(k_ref[pl.ds(row0, blk), cols[h]], qs[h])
        s = jnp.where(kpos <= qpos, s, -jnp.inf)
        m0 = jnp.max(s, axis=0, keepdims=True)
        p = jnp.exp2(s - m0)
        m_init.append(m0)
        l_init.append(jnp.sum(p, axis=0, keepdims=True))
        acc_init.append(_dot_tn(v_ref[pl.ds(row0, blk), cols[h]], p.astype(BF16)))
    for h in range(heads):
        off_ref[h] = off_init[h]
        acc_ref[h] = acc_init[h]

    def issue_scores(j, dst_ref):
        r = pl.multiple_of(j * blk, blk)
        maxima = []
        for h in range(heads):
            s = _dot_nt(k_ref[pl.ds(r, blk), cols[h]], q_ref[:, cols[h]])
            dst_ref[h] = s
            maxima.append(jnp.max(s, axis=0, keepdims=True))
        return tuple(maxima)

    def half_step(j, ms, ls, cur_max, cur_ref, nxt_ref):
        jc = jnp.minimum(j, qi - 1)
        r0 = pl.multiple_of(jc * blk, blk)
        live = j < qi
        offs = [jnp.where(live, off_ref[h, pl.ds(jc, 1), :], jnp.inf) for h in range(heads)]
        nxt_max = issue_scores(jnp.minimum(j + 1, qi - 1), nxt_ref)
        new_ms, new_ls = [], []
        for h in range(heads):
            sj = cur_ref[h]
            m_new = jnp.maximum(ms[h], cur_max[h] - offs[h])
            alpha = jnp.exp2(ms[h] - m_new)
            pj = jnp.exp2(sj - (m_new + offs[h]))
            new_ms.append(m_new)
            new_ls.append(alpha * ls[h] + jnp.sum(pj, axis=0, keepdims=True))
            acc_ref[h] = alpha * acc_ref[h] + _dot_tn(v_ref[pl.ds(r0, blk), cols[h]], pj.astype(BF16))
        return tuple(new_ms), tuple(new_ls), nxt_max

    def body(jj, carry):
        ms, ls, max_a = carry
        ms, ls, max_b = half_step(2 * jj, ms, ls, max_a, sa_ref, sb_ref)
        return half_step(2 * jj + 1, ms, ls, max_b, sb_ref, sa_ref)

    max_0 = issue_scores(0, sa_ref)
    _, ls, _ = lax.fori_loop(0, (qi + 1) // 2, body, (tuple(m_init), tuple(l_init), max_0))
    outs = [(acc_ref[h] * (1.0 / ls[h])).T for h in range(heads)]
    gates = _silu(z_ref[...].astype(F32))
    o_ref[...] = (jnp.concatenate(outs, axis=1) * gates).astype(o_ref.dtype)


def _moba(kvqz, *, batch, seq, heads=4):
    blk = MOBA_BLOCK
    nb = seq // blk
    w = heads * ATT_HEAD_DIM
    ngroups = ATT_HEADS // heads
    return pl.pallas_call(
        functools.partial(_moba_kernel, heads=heads),
        grid=(batch, ngroups, nb),
        in_specs=[
            pl.BlockSpec((blk, w), lambda b, h, i: (b * nb + i, 2 * ngroups + h)),
            pl.BlockSpec((seq, w), lambda b, h, i: (b, h)),
            pl.BlockSpec((seq, w), lambda b, h, i: (b, ngroups + h)),
            pl.BlockSpec((blk, w), lambda b, h, i: (b * nb + i, 3 * ngroups + h)),
        ],
        out_specs=pl.BlockSpec((blk, w), lambda b, h, i: (b * nb + i, h)),
        out_shape=jax.ShapeDtypeStruct((batch * seq, ATT_WIDTH), BF16),
        scratch_shapes=[pltpu.VMEM((nb, w), F32),
                        pltpu.VMEM((heads, nb, blk), F32),
                        pltpu.VMEM((heads, ATT_HEAD_DIM, blk), F32),
                        pltpu.VMEM((heads, blk, blk), F32),
                        pltpu.VMEM((heads, blk, blk), F32)],
        compiler_params=_cparams("parallel", "parallel", "arbitrary"),
        name="moba",
    )(kvqz, kvqz, kvqz, kvqz)


def _final_kernel(o_ref, w_ref, r_ref, g_ref, out_ref):
    x = r_ref[...] + _dot(o_ref[...], w_ref[...])
    out_ref[...] = x * _rms_scale(x) * g_ref[...]


def _final(o, w, res, g, *, tm=512):
    t, k = o.shape
    n = w.shape[1]
    return pl.pallas_call(
        _final_kernel,
        grid=(t // tm,),
        in_specs=[
            pl.BlockSpec((tm, k), lambda i: (i, 0)),
            pl.BlockSpec((k, n), lambda i: (0, 0)),
            pl.BlockSpec((tm, n), lambda i: (i, 0)),
            pl.BlockSpec((1, n), lambda i: (0, 0)),
        ],
        out_specs=pl.BlockSpec((tm, n), lambda i: (i, 0)),
        out_shape=jax.ShapeDtypeStruct((t, n), F32),
        compiler_params=_cparams("parallel"),
        name="final",
    )(o, w, res, g)


def _rope_tables(seq):
    hd = ATT_HEAD_DIM
    inv_freq = ROPE_THETA ** (-np.arange(0, hd, 2, dtype=np.float64) / hd)
    ang = np.arange(seq, dtype=np.float64)[:, None] * inv_freq[None, :]
    cos, sin = np.cos(ang), np.sin(ang)
    return (jnp.asarray(np.concatenate([cos, cos], axis=1), F32),
            jnp.asarray(np.concatenate([-sin, sin], axis=1), F32))


def kernel(x, a_norm_g, a_w_in, a_conv_w, a_conv_b, a_dt_bias, a_A_log, a_D, a_gnorm_g, a_w_out,
           kv_norm_g, w_kv, b_norm_g, b_w_in, b_w_out, final_norm_g):
    batch, seq, d = x.shape
    assert d == D_MODEL and seq % SSD_CHUNK == 0 and seq % MOBA_BLOCK == 0
    assert a_w_in.shape[0] == 1 and b_w_in.shape[0] == 1
    t = batch * seq
    G, R, N = SSM_GROUPS, HEADS_PER_GROUP, SSM_STATE
    x2d = x.reshape(t, d)

    w_in = a_w_in[0]
    n_main = 2 * D_INNER + 2 * G * N
    w_t = w_in.T.astype(BF16)
    w_dt_t = w_t[n_main:, :]

    def group_major(p):
        rows = p.shape[0]
        px = p[:, :D_INNER].reshape(rows, G, GROUP_WIDTH)
        pb = p[:, D_INNER:D_INNER + G * N].reshape(rows, G, N)
        pcc = p[:, D_INNER + G * N:].reshape(rows, G, N)
        return jnp.transpose(jnp.concatenate([px, pb, pcc], axis=2), (1, 0, 2))

    cw = group_major(a_conv_w[0])
    cb = group_major(a_conv_b[0][None, :])
    dt_bias = a_dt_bias[0].reshape(G, R)
    a_log = a_A_log[0].reshape(G, R)
    pr = jnp.stack([dt_bias, a_log], axis=2)
    dexp = jnp.repeat(a_D[0], SSM_HEAD_DIM).reshape(G, 1, GROUP_WIDTH)
    gn = a_gnorm_g[0].reshape(G, 1, GROUP_WIDTH)
    tril = jnp.tril(jnp.ones((SSD_CHUNK, SSD_CHUNK), BF16))
    expand = jnp.repeat(jnp.eye(R, dtype=BF16), SSM_HEAD_DIM, axis=1)
    shift = jnp.concatenate([jnp.eye(SSD_CHUNK, k=-dd, dtype=BF16) for dd in range(1, CONV_K)], axis=0)

    proj, dt_t = _a_in_proj(x2d, a_norm_g[0][None, :], w_t, w_dt_t, n=n_main)
    y = _ssd(proj, dt_t, cw, cb, pr, dexp, gn, tril, expand, shift, batch=batch, seq=seq)
    x1, xn = _proj_res(y, a_w_out[0].astype(BF16), x2d)

    cos, sin = _rope_tables(seq)
    w_kv_g = (kv_norm_g[:, None] * w_kv).astype(BF16)
    w_qz_g = (b_norm_g[0][:, None] * b_w_in[0]).astype(BF16)
    kvqz = _b_proj(xn, w_kv_g, w_qz_g, cos, sin, seq=seq)
    o = _moba(kvqz, batch=batch, seq=seq)
    out = _final(o, b_w_out[0].astype(BF16), x1, final_norm_g[None, :])
    return out.reshape(batch, seq, d)
```
